```python
import math, functools
import jax, jax.numpy as jnp
from jax import lax
import numpy as np


D_MODEL = 1024
BATCH = 4
SEQ = 4096
DEPTH = 4
DEC_BATCH = 128
DEC_SEQ = 8
PAST_LEN = 8192
PAGE_SIZE = 128

MIX_DIM = D_MODEL
POOL_WINDOWS = (2, 4, 8, 16)
POOL_DIM = MIX_DIM // 4
POOL_GROUP = POOL_DIM // len(POOL_WINDOWS)
POOL_HIST = max(POOL_WINDOWS) - 1
SSD_HEADDIM = 64
SSD_HEADS = (MIX_DIM * 3 // 8) // SSD_HEADDIM
SSD_DIM = SSD_HEADS * SSD_HEADDIM
SSD_GROUPS = 2
SSD_STATE = 64
SSD_CONV = 4
SSD_CONV_DIM = SSD_DIM + 2 * SSD_GROUPS * SSD_STATE
SSD_CHUNK = 128
MLA_HEADS = 6
MLA_NOPE = 64
MLA_ROPE = 32
MLA_V = 64
MLA_Q_LORA = 256
MLA_KV_LORA = 128
MLA_DIM = MLA_HEADS * MLA_V
MLA_SCALE = (MLA_NOPE + MLA_ROPE) ** -0.5
ROPE_THETA = 10000.0
Q_BLOCK = 128
FFN_DIM = 4 * D_MODEL
NORM_EPS = 1e-6
IN_WIDTHS = (POOL_DIM, SSD_DIM, SSD_CONV_DIM, SSD_HEADS, MLA_Q_LORA, MLA_KV_LORA, MLA_ROPE)
IN_DIM = sum(IN_WIDTHS)

kernel_name = 'pool_ssd_mla_hybrid_step'


def rmsnorm(x, g):
    xf = x.astype(jnp.float32)
    xf = xf * lax.rsqrt(jnp.mean(xf * xf, axis=-1, keepdims=True) + NORM_EPS)
    return (xf * g.astype(jnp.float32)).astype(x.dtype)


def rope(x, pos):
    half = MLA_ROPE // 2
    inv = ROPE_THETA ** (-jnp.arange(half, dtype=jnp.float32) * (2.0 / MLA_ROPE))
    ang = pos.astype(jnp.float32)[:, None] * inv[None, :]
    shape = (1, pos.shape[0]) + (1,) * (x.ndim - 3) + (half,)
    cos = jnp.cos(ang).reshape(shape)
    sin = jnp.sin(ang).reshape(shape)
    xf = x.astype(jnp.float32)
    x1, x2 = xf[..., :half], xf[..., half:]
    return jnp.concatenate([x1 * cos - x2 * sin, x1 * sin + x2 * cos], axis=-1).astype(x.dtype)


def split_in_proj(proj):
    parts, start = [], 0
    for w in IN_WIDTHS:
        parts.append(proj[..., start:start + w])
        start += w
    return parts


def pool_mix(u, hist, pos, pool_w, pool_scale):
    L = u.shape[1]
    ext = jnp.concatenate([hist.astype(u.dtype), u], axis=1)
    extf = ext.astype(jnp.float32)
    cs = jnp.pad(jnp.cumsum(extf, axis=1), ((0, 0), (1, 0), (0, 0)))
    uf = extf[:, POOL_HIST:]
    outs = []
    for g, w in enumerate(POOL_WINDOWS):
        c0, c1 = g * POOL_GROUP, (g + 1) * POOL_GROUP
        hi = cs[:, POOL_HIST + 1:POOL_HIST + 1 + L, c0:c1]
        lo = cs[:, POOL_HIST + 1 - w:POOL_HIST + 1 - w + L, c0:c1]
        cnt = jnp.minimum(w, pos + 1).astype(jnp.float32)[None, :, None]
        m = (hi - lo) / cnt - uf[..., c0:c1]
        outs.append(m @ pool_w[g].astype(jnp.float32))
    out = jnp.concatenate(outs, axis=-1) * pool_scale.astype(jnp.float32)
    return out.astype(u.dtype), ext[:, -POOL_HIST:]


def causal_dwconv(x, hist, w, b):
    ext = jnp.concatenate([hist.astype(x.dtype), x], axis=1)
    y = lax.conv_general_dilated(ext, w[:, None, :].astype(x.dtype), (1,), 'VALID',
                                 dimension_numbers=('NWC', 'WIO', 'NWC'),
                                 feature_group_count=x.shape[-1])
    return y + b.astype(x.dtype), ext[:, -(SSD_CONV - 1):]


def ssd_scan(x, dt, A, Bm, Cm, h0):
    b, l, H, P = x.shape
    G, N = Bm.shape[2], Bm.shape[3]
    R = H // G
    q = min(SSD_CHUNK, l)
    c = l // q
    xc = x.reshape(b, c, q, G, R, P)
    dtc = dt.reshape(b, c, q, G, R)
    Bc = Bm.reshape(b, c, q, G, N)
    Cc = Cm.reshape(b, c, q, G, N)
    a = jnp.moveaxis(dtc * A.reshape(G, R), 2, -1)
    a_cs = jnp.cumsum(a, axis=-1)
    seg = a_cs[..., :, None] - a_cs[..., None, :]
    causal = jnp.tril(jnp.ones((q, q), dtype=bool))
    Lmat = jnp.exp(jnp.where(causal, seg, -jnp.inf))
    xdt = xc * dtc[..., None]
    cb = jnp.einsum('bclgn,bcsgn->bcgls', Cc, Bc)
    M = cb[:, :, :, None] * Lmat
    y_diag = jnp.einsum('bcgrls,bcsgrp->bclgrp', M, xdt)
    decay_states = jnp.exp(a_cs[..., -1:] - a_cs)
    states = jnp.einsum('bcsgn,bcgrs,bcsgrp->bcgrpn', Bc, decay_states, xdt)
    chunk_decay = jnp.exp(a_cs[..., -1])

    def step(h, inp):
        st, dec = inp
        return h * dec[..., None, None] + st, h

    hT, h_prev = lax.scan(step, h0.reshape(b, G, R, P, N),
                          (jnp.moveaxis(states, 1, 0), jnp.moveaxis(chunk_decay, 1, 0)))
    h_prev = jnp.moveaxis(h_prev, 0, 1)
    y_off = jnp.einsum('bclgn,bcgrpn,bcgrl->bclgrp', Cc, h_prev, jnp.exp(a_cs))
    y = (y_diag + y_off).reshape(b, l, H, P)
    return y, hT.reshape(b, H, P, N)


def ssd_mix(z, xbc, dt_raw, conv_hist, h0, p):
    b, L, _ = z.shape
    xbc, new_conv = causal_dwconv(xbc, conv_hist, p['conv_w'], p['conv_b'])
    xbc = jax.nn.silu(xbc.astype(jnp.float32))
    xs = xbc[..., :SSD_DIM].reshape(b, L, SSD_HEADS, SSD_HEADDIM)
    Bm = xbc[..., SSD_DIM:SSD_DIM + SSD_GROUPS * SSD_STATE].reshape(b, L, SSD_GROUPS, SSD_STATE)
    Cm = xbc[..., SSD_DIM + SSD_GROUPS * SSD_STATE:].reshape(b, L, SSD_GROUPS, SSD_STATE)
    dt = jax.nn.softplus(dt_raw.astype(jnp.float32) + p['dt_bias'].astype(jnp.float32))
    A = -jnp.exp(p['a_log'].astype(jnp.float32))
    y, hT = ssd_scan(xs, dt, A, Bm, Cm, h0.astype(jnp.float32))
    y = y + p['d_skip'].astype(jnp.float32)[:, None] * xs
    y = y.reshape(b, L, SSD_DIM) * jax.nn.silu(z.astype(jnp.float32))
    return rmsnorm(y, p['ssd_norm_g']).astype(z.dtype), new_conv, hT.astype(h0.dtype)


def mla_prompt_attn(q_lat, q_pe, ckv, kpe):
    b, S, H, C = q_lat.shape
    qb = min(Q_BLOCK, S)
    kpos = jnp.arange(S)

    def blk(i):
        s0 = i * qb
        ql = lax.dynamic_slice_in_dim(q_lat, s0, qb, axis=1)
        qp = lax.dynamic_slice_in_dim(q_pe, s0, qb, axis=1)
        sc = (jnp.einsum('bthc,bsc->bhts', ql, ckv) +
              jnp.einsum('bthr,bsr->bhts', qp, kpe)).astype(jnp.float32) * MLA_SCALE
        qpos = s0 + jnp.arange(qb)
        sc = jnp.where(kpos[None, :] <= qpos[:, None], sc, -jnp.inf)
        pr = jax.nn.softmax(sc, axis=-1).astype(ckv.dtype)
        return jnp.einsum('bhts,bsc->bthc', pr, ckv)

    o = lax.map(blk, jnp.arange(S // qb))
    return jnp.moveaxis(o, 0, 1).reshape(b, S, H, C)


def mla_sample_attn(q_lat, q_pe, ckv, kpe, ckv_past, kpe_past):
    t = q_lat.shape[1]
    sc_past = (jnp.einsum('bthc,bsc->bhts', q_lat, ckv_past) +
               jnp.einsum('bthr,bsr->bhts', q_pe, kpe_past)).astype(jnp.float32) * MLA_SCALE
    sc_new = (jnp.einsum('bthc,bsc->bhts', q_lat, ckv) +
              jnp.einsum('bthr,bsr->bhts', q_pe, kpe)).astype(jnp.float32) * MLA_SCALE
    sc_new = jnp.where(jnp.tril(jnp.ones((t, t), dtype=bool)), sc_new, -jnp.inf)
    pr = jax.nn.softmax(jnp.concatenate([sc_past, sc_new], axis=-1), axis=-1).astype(ckv.dtype)
    P = ckv_past.shape[1]
    return (jnp.einsum('bhts,bsc->bthc', pr[..., :P], ckv_past) +
            jnp.einsum('bhts,bsc->bthc', pr[..., P:], ckv))


def mla_mix(cq, ckv, kpe, pos, p, attend):
    b, L, _ = cq.shape
    q = (rmsnorm(cq, p['q_norm_g']) @ p['w_uq']).reshape(b, L, MLA_HEADS, MLA_NOPE + MLA_ROPE)
    q_nope = q[..., :MLA_NOPE]
    q_pe = rope(q[..., MLA_NOPE:], pos)
    ckv = rmsnorm(ckv, p['kv_norm_g'])
    kpe = rope(kpe, pos)
    q_lat = jnp.einsum('blhd,chd->blhc', q_nope, p['w_uk'])
    o_lat = attend(q_lat, q_pe, ckv, kpe)
    o = jnp.einsum('blhc,chd->blhd', o_lat, p['w_uv']).reshape(b, L, MLA_DIM)
    return rmsnorm(o, p['mla_out_g']), ckv, kpe


def mixer(h, p, pos0, pool_hist, conv_hist, ssm_h0, attend):
    L = h.shape[1]
    pos = pos0 + jnp.arange(L, dtype=jnp.int32)
    u, z, xbc, dt_raw, cq, ckv, kpe = split_in_proj(h @ p['w_in'])
    pool_out, new_pool = pool_mix(u, pool_hist, pos, p['pool_w'], p['pool_scale'])
    ssd_out, new_conv, new_ssm = ssd_mix(z, xbc, dt_raw, conv_hist, ssm_h0, p)
    mla_out, ckv_n, kpe_r = mla_mix(cq, ckv, kpe, pos, p, attend)
    y = jnp.concatenate([pool_out, ssd_out.astype(h.dtype), mla_out], axis=-1) @ p['w_out']
    return y, ckv_n, kpe_r, new_pool, new_conv, new_ssm


def ffn(h, w_up, w_down):
    a = jax.nn.relu(h @ w_up)
    return (a * a) @ w_down


def setup_inputs(seed: int = 0) -> dict:
    key = jax.random.key(seed)
    ks = iter(jax.random.split(key, 40))
    f32 = jnp.float32
    n_pages = PAST_LEN // PAGE_SIZE
    used = DEC_BATCH * n_pages
    n_phys = used + max(1, used // 4)

    def nrm(shape, scale):
        return jax.random.normal(next(ks), shape, f32) * scale

    def gain(shape):
        return 1.0 + 0.1 * jax.random.normal(next(ks), shape, f32)

    x_prompt = nrm((BATCH, SEQ, D_MODEL), 1.0)
    x_sample = nrm((DEC_BATCH, DEC_SEQ, D_MODEL), 1.0)
    cache_kv_latent = nrm((DEPTH, n_phys, PAGE_SIZE, MLA_KV_LORA), 1.0)
    cache_k_rope = nrm((DEPTH, n_phys, PAGE_SIZE, MLA_ROPE), 1.0)
    state_pool = nrm((DEPTH, DEC_BATCH, POOL_HIST, POOL_DIM), 1.0)
    state_conv = nrm((DEPTH, DEC_BATCH, SSD_CONV - 1, SSD_CONV_DIM), 1.0)
    state_ssm = nrm((DEPTH, DEC_BATCH, SSD_HEADS, SSD_HEADDIM, SSD_STATE), 0.3)
    page_table = jax.random.permutation(next(ks), n_phys)[:used].reshape(DEC_BATCH, n_pages).astype(jnp.int32)

    dt0 = jnp.exp(jax.random.uniform(next(ks), (DEPTH, SSD_HEADS), f32) *
                  (math.log(0.1) - math.log(1e-3)) + math.log(1e-3))
    dt_bias = dt0 + jnp.log(-jnp.expm1(-dt0))
    a_log = jnp.log(jax.random.uniform(next(ks), (DEPTH, SSD_HEADS), f32, 1.0, 16.0))

    return {
        'x_prompt': x_prompt,
        'x_sample': x_sample,
        'cache_kv_latent': cache_kv_latent,
        'cache_k_rope': cache_k_rope,
        'state_pool': state_pool,
        'state_conv': state_conv,
        'state_ssm': state_ssm,
        'page_table': page_table,
        'norm_mix_g': gain((DEPTH, D_MODEL)),
        'w_in': nrm((DEPTH, D_MODEL, IN_DIM), D_MODEL ** -0.5),
        'pool_w': nrm((DEPTH, len(POOL_WINDOWS), POOL_GROUP, POOL_GROUP), POOL_GROUP ** -0.5),
        'pool_scale': gain((DEPTH, POOL_DIM)),
        'conv_w': nrm((DEPTH, SSD_CONV, SSD_CONV_DIM), SSD_CONV ** -0.5),
        'conv_b': nrm((DEPTH, SSD_CONV_DIM), 0.02),
        'dt_bias': dt_bias,
        'a_log': a_log,
        'd_skip': gain((DEPTH, SSD_HEADS)),
        'ssd_norm_g': gain((DEPTH, SSD_DIM)),
        'q_norm_g': gain((DEPTH, MLA_Q_LORA)),
        'w_uq': nrm((DEPTH, MLA_Q_LORA, MLA_HEADS * (MLA_NOPE + MLA_ROPE)), MLA_Q_LORA ** -0.5),
        'kv_norm_g': gain((DEPTH, MLA_KV_LORA)),
        'w_uk': nrm((DEPTH, MLA_KV_LORA, MLA_HEADS, MLA_NOPE), MLA_KV_LORA ** -0.5),
        'w_uv': nrm((DEPTH, MLA_KV_LORA, MLA_HEADS, MLA_V), MLA_KV_LORA ** -0.5),
        'mla_out_g': gain((DEPTH, MLA_DIM)),
        'w_out': nrm((DEPTH, MIX_DIM, D_MODEL), MIX_DIM ** -0.5),
        'norm_ffn_g': gain((DEPTH, D_MODEL)),
        'w_up': nrm((DEPTH, D_MODEL, FFN_DIM), D_MODEL ** -0.5),
        'w_down': nrm((DEPTH, FFN_DIM, D_MODEL), FFN_DIM ** -0.5),
        'final_norm_g': gain((D_MODEL,)),
    }


def reference(x_prompt, x_sample, cache_kv_latent, cache_k_rope, state_pool, state_conv, state_ssm,
              page_table, norm_mix_g, w_in, pool_w, pool_scale, conv_w, conv_b, dt_bias, a_log,
              d_skip, ssd_norm_g, q_norm_g, w_uq, kv_norm_g, w_uk, w_uv, mla_out_g, w_out,
              norm_ffn_g, w_up, w_down, final_norm_g):
    past_len = page_table.shape[1] * PAGE_SIZE
    bp = x_prompt.shape[0]
    db = x_sample.shape[0]
    xp, xs = x_prompt, x_sample
    p_kv, p_kr, p_pool, p_conv, p_ssm = [], [], [], [], []
    s_kv, s_kr, s_pool, s_conv, s_ssm = [], [], [], [], []
    for l in range(DEPTH):
        p = {'w_in': w_in[l], 'pool_w': pool_w[l], 'pool_scale': pool_scale[l],
             'conv_w': conv_w[l], 'conv_b': conv_b[l], 'dt_bias': dt_bias[l], 'a_log': a_log[l],
             'd_skip': d_skip[l], 'ssd_norm_g': ssd_norm_g[l], 'q_norm_g': q_norm_g[l],
             'w_uq': w_uq[l], 'kv_norm_g': kv_norm_g[l], 'w_uk': w_uk[l], 'w_uv': w_uv[l],
             'mla_out_g': mla_out_g[l], 'w_out': w_out[l]}

        zp_pool = jnp.zeros((bp, POOL_HIST, POOL_DIM), xp.dtype)
        zp_conv = jnp.zeros((bp, SSD_CONV - 1, SSD_CONV_DIM), xp.dtype)
        zp_ssm = jnp.zeros((bp, SSD_HEADS, SSD_HEADDIM, SSD_STATE), xp.dtype)
        y, ckv, kpe, npool, nconv, nssm = mixer(rmsnorm(xp, norm_mix_g[l]), p, 0,
                                                zp_pool, zp_conv, zp_ssm, mla_prompt_attn)
        xp = xp + y
        xp = xp + ffn(rmsnorm(xp, norm_ffn_g[l]), w_up[l], w_down[l])
        p_kv.append(ckv); p_kr.append(kpe); p_pool.append(npool); p_conv.append(nconv); p_ssm.append(nssm)

        ckv_past = jnp.take(cache_kv_latent[l], page_table, axis=0).reshape(db, past_len, MLA_KV_LORA)
        kpe_past = jnp.take(cache_k_rope[l], page_table, axis=0).reshape(db, past_len, MLA_ROPE)
        attend = functools.partial(mla_sample_attn, ckv_past=ckv_past.astype(xs.dtype),
                                   kpe_past=kpe_past.astype(xs.dtype))
        y, ckv, kpe, npool, nconv, nssm = mixer(rmsnorm(xs, norm_mix_g[l]), p, past_len,
                                                state_pool[l], state_conv[l], state_ssm[l], attend)
        xs = xs + y
        xs = xs + ffn(rmsnorm(xs, norm_ffn_g[l]), w_up[l], w_down[l])
        s_kv.append(ckv); s_kr.append(kpe); s_pool.append(npool); s_conv.append(nconv); s_ssm.append(nssm)

    y_prompt = rmsnorm(xp, final_norm_g)
    y_sample = rmsnorm(xs, final_norm_g)
    return (y_prompt, y_sample,
            jnp.stack(p_kv), jnp.stack(p_kr), jnp.stack(p_pool), jnp.stack(p_conv), jnp.stack(p_ssm),
            jnp.stack(s_kv), jnp.stack(s_kr), jnp.stack(s_pool), jnp.stack(s_conv), jnp.stack(s_ssm))
```

```python
import functools

import jax
import jax.numpy as jnp
from jax import lax
from jax.experimental import pallas as pl
from jax.experimental.pallas import tpu as pltpu

D_MODEL = 1024
POOL_WINDOWS = (2, 4, 8, 16)
POOL_DIM = 256
POOL_GROUP = 64
POOL_HIST = 15
SSD_HEADDIM = 64
SSD_HEADS = 6
SSD_DIM = 384
SSD_GROUPS = 2
SSD_STATE = 64
SSD_CONV = 4
SSD_CONV_DIM = 640
SSD_CHUNK = 128
MLA_HEADS = 6
MLA_NOPE = 64
MLA_ROPE = 32
MLA_V = 64
MLA_Q_LORA = 256
MLA_KV_LORA = 128
MLA_DIM = 384
MLA_SCALE = (MLA_NOPE + MLA_ROPE) ** -0.5
ROPE_THETA = 10000.0
FFN_DIM = 4096
NORM_EPS = 1e-6
PAGE_SIZE = 128
IN_WIDTHS = (POOL_DIM, SSD_DIM, SSD_CONV_DIM, SSD_HEADS, MLA_Q_LORA, MLA_KV_LORA, MLA_ROPE)

LANES = 128
SUBLANES = 8
HIST_ROWS = 16
CONV_ROWS = 8
QK_WIDTH = 2 * LANES
VMEM_LIMIT = 48 * 1024 * 1024

BF = jnp.bfloat16
F32 = jnp.float32


def _cparams(n_axes):
    return pltpu.CompilerParams(dimension_semantics=("arbitrary",) * n_axes,
                                vmem_limit_bytes=VMEM_LIMIT)


def _rms(x, g):
    return x * lax.rsqrt(jnp.mean(x * x, axis=-1, keepdims=True) + NORM_EPS) * g


def _dot(a, b):
    return jnp.dot(a, b, preferred_element_type=F32)


def _dot_nt(a, b):
    return lax.dot_general(a, b, (((1,), (1,)), ((), ())), preferred_element_type=F32)


def _row_tile(n, target):
    t = min(n, target)
    while n % t:
        t //= 2
    return t


IN_OUT_WIDTHS = (POOL_DIM, SSD_DIM, SSD_CONV_DIM, MLA_Q_LORA, MLA_KV_LORA, LANES, LANES, LANES)


def _in_proj_kernel(x_ref, g_ref, w_ref, *out_refs):
    h = _rms(x_ref[...], g_ref[...]).astype(BF)
    off = 0
    for o_ref in out_refs:
        width = o_ref.shape[-1]
        o_ref[...] = _dot(h, w_ref[:, off:off + width])
        off += width


def _in_proj(x, g, w):
    n = x.shape[0]
    tm = _row_tile(n, 512)
    wtot = w.shape[1]
    return pl.pallas_call(
        _in_proj_kernel,
        grid=(n // tm,),
        in_specs=[pl.BlockSpec((tm, D_MODEL), lambda i: (i, 0)),
                  pl.BlockSpec((1, D_MODEL), lambda i: (0, 0)),
                  pl.BlockSpec((D_MODEL, wtot), lambda i: (0, 0))],
        out_specs=[pl.BlockSpec((tm, wd), lambda i: (i, 0)) for wd in IN_OUT_WIDTHS],
        out_shape=[jax.ShapeDtypeStruct((n, wd), F32) for wd in IN_OUT_WIDTHS],
        compiler_params=_cparams(1),
        name="in_proj",
    )(x, g, w)


def _pool_kernel(u_ref, hist_ref, w_ref, scale_ref, out_ref, newhist_ref, carry_sc, *, pos0, tile):
    t = pl.program_id(1)

    @pl.when(t == 0)
    def _():
        carry_sc[...] = hist_ref[0]

    u = u_ref[0]
    ext = jnp.concatenate([carry_sc[...], u], axis=0)
    s2 = ext + pltpu.roll(ext, 1, 0)
    s4 = s2 + pltpu.roll(s2, 2, 0)
    s8 = s4 + pltpu.roll(s4, 4, 0)
    s16 = s8 + pltpu.roll(s8, 8, 0)
    lane = lax.broadcasted_iota(jnp.int32, (1, POOL_DIM), 1)
    g0, g1, g2 = lane < POOL_GROUP, lane < 2 * POOL_GROUP, lane < 3 * POOL_GROUP
    wsum = jnp.where(g0, s2, jnp.where(g1, s4, jnp.where(g2, s8, s16)))[HIST_ROWS:]
    width = jnp.where(g0, 2, jnp.where(g1, 4, jnp.where(g2, 8, 16)))
    pos = pos0 + t * tile + lax.broadcasted_iota(jnp.int32, (tile, 1), 0)
    cnt = jnp.minimum(width, pos + 1).astype(F32)
    m = wsum / cnt - u
    out = _dot(m.astype(BF), w_ref[...]) * scale_ref[...]
    out_ref[0] = out.astype(out_ref.dtype)
    last = ext[tile:tile + HIST_ROWS]
    carry_sc[...] = last
    newhist_ref[0] = last


def _pool(u, hist, wbd, scale, pos0, out_dtype):
    b, l, _ = u.shape
    tile = _row_tile(l, 256)
    return pl.pallas_call(
        functools.partial(_pool_kernel, pos0=pos0, tile=tile),
        grid=(b, l // tile),
        in_specs=[pl.BlockSpec((1, tile, POOL_DIM), lambda i, t: (i, t, 0)),
                  pl.BlockSpec((1, HIST_ROWS, POOL_DIM), lambda i, t: (i, 0, 0)),
                  pl.BlockSpec((POOL_DIM, POOL_DIM), lambda i, t: (0, 0)),
                  pl.BlockSpec((1, POOL_DIM), lambda i, t: (0, 0))],
        out_specs=[pl.BlockSpec((1, tile, POOL_DIM), lambda i, t: (i, t, 0)),
                   pl.BlockSpec((1, HIST_ROWS, POOL_DIM), lambda i, t: (i, 0, 0))],
        out_shape=[jax.ShapeDtypeStruct((b, l, POOL_DIM), out_dtype),
                   jax.ShapeDtypeStruct((b, HIST_ROWS, POOL_DIM), F32)],
        scratch_shapes=[pltpu.VMEM((HIST_ROWS, POOL_DIM), F32)],
        compiler_params=_cparams(2),
        name="pool_mix",
    )(u, hist, wbd, scale)


def _expand_heads(x):
    q = x.shape[0]
    col = lax.broadcasted_iota(jnp.int32, (1, SSD_DIM), 1)
    out = jnp.zeros((q, SSD_DIM), F32)
    for h in range(SSD_HEADS):
        sel = (col >= h * SSD_HEADDIM) & (col < (h + 1) * SSD_HEADDIM)
        out = jnp.where(sel, jnp.broadcast_to(x[:, h:h + 1], (q, SSD_DIM)), out)
    return out


def _ssd_kernel(z_ref, xbc_ref, dt_ref, hist_ref, h0_ref, convw_ref, convb_ref, dtb_ref, alog_ref,
                dskip_ref, ng_ref, y_ref, newconv_ref, ht_ref, ext_sc, hbig_sc, *, rows, bb):
    c = pl.program_id(1)
    q = SSD_CHUNK
    pad = q - rows

    def padded(x):
        if pad == 0:
            return x
        return jnp.concatenate([x, jnp.zeros((pad, x.shape[1]), x.dtype)], axis=0)

    row = lax.broadcasted_iota(jnp.int32, (q, 1), 0)
    col_t = lax.broadcasted_iota(jnp.int32, (1, q), 1)
    causal = row >= col_t
    lane = lax.broadcasted_iota(jnp.int32, (1, LANES), 1)
    col384 = lax.broadcasted_iota(jnp.int32, (1, SSD_DIM), 1)
    row384 = lax.broadcasted_iota(jnp.int32, (SSD_DIM, 1), 0)
    upper_rows = row384 >= SSD_DIM // SSD_GROUPS
    state_valid = jnp.where(upper_rows, 1, 0) == jnp.where(lane >= SSD_STATE, 1, 0)
    a_neg = -jnp.exp(alog_ref[...])

    def per_seq(i, carry):
        @pl.when(c == 0)
        def _():
            ext_sc[i, 0:CONV_ROWS, :] = hist_ref[i]
            h0 = h0_ref[i]
            hbig_sc[i] = jnp.where(upper_rows, pltpu.roll(h0, SSD_STATE, 1), h0)

        ext_sc[i, CONV_ROWS:CONV_ROWS + q, :] = padded(xbc_ref[i])
        conv = convb_ref[...]
        for k in range(SSD_CONV):
            start = CONV_ROWS - (SSD_CONV - 1) + k
            conv = conv + convw_ref[k:k + 1, :] * ext_sc[i, start:start + q, :]
        tail = ext_sc[i, rows:rows + CONV_ROWS, :]
        newconv_ref[i] = tail
        ext_sc[i, 0:CONV_ROWS, :] = tail

        act = conv * jax.nn.sigmoid(conv)
        xs = act[:, :SSD_DIM]
        bmat = act[:, SSD_DIM:SSD_DIM + LANES]
        cmat = act[:, SSD_DIM + LANES:]

        dtr = padded(dt_ref[i]) + dtb_ref[...]
        dt = jnp.maximum(dtr, 0.0) + jnp.log(1.0 + jnp.exp(-jnp.abs(dtr)))
        dt = jnp.where(row < rows, dt, 0.0)
        a_cs = dt * a_neg
        k = 1
        while k < q:
            a_cs = a_cs + jnp.where(row >= k, pltpu.roll(a_cs, k, 0), 0.0)
            k *= 2
        a_cs_t = a_cs.T
        a_last = a_cs[q - 1:q, :]
        decay_to_end = jnp.exp(a_last - a_cs)
        decay_from_start = jnp.exp(a_cs)
        chunk_decay = jnp.exp(a_last)

        dt_cols = _expand_heads(dt)
        xdt = xs * dt_cols
        xdec = xs * _expand_heads(dt * decay_to_end)

        bmat_b = bmat.astype(BF)
        cb = [_dot_nt(jnp.where((lane >= g * SSD_STATE) & (lane < (g + 1) * SSD_STATE), cmat, 0.0).astype(BF),
                      bmat_b) for g in range(SSD_GROUPS)]
        y = jnp.zeros((q, SSD_DIM), F32)
        for h in range(SSD_HEADS):
            seg = a_cs[:, h:h + 1] - a_cs_t[h:h + 1, :]
            lmat = jnp.exp(jnp.where(causal, seg, -jnp.inf))
            m = cb[h // (SSD_HEADS // SSD_GROUPS)] * lmat
            sel = (col384 >= h * SSD_HEADDIM) & (col384 < (h + 1) * SSD_HEADDIM)
            y = y + _dot(m.astype(BF), jnp.where(sel, xdt, 0.0).astype(BF))

        hbig = hbig_sc[i]
        y_off = _dot_nt(cmat.astype(BF), hbig.astype(BF)) * _expand_heads(decay_from_start)
        y = y + y_off + dskip_ref[...] * xs
        zf = padded(z_ref[i])
        gated = y * (zf * jax.nn.sigmoid(zf))
        y_ref[i] = _rms(gated, ng_ref[...])[:rows].astype(y_ref.dtype)

        new_states = _dot(xdec.T.astype(BF), bmat_b)
        cd_rows = _expand_heads(jnp.broadcast_to(chunk_decay, (q, LANES))).T[:, :LANES]
        hnew = hbig * cd_rows + jnp.where(state_valid, new_states, 0.0)
        hbig_sc[i] = hnew
        ht_ref[i] = jnp.where(upper_rows, pltpu.roll(hnew, SSD_STATE, 1), hnew)[:, :SSD_STATE]
        return carry

    lax.fori_loop(0, bb, per_seq, 0)


def _ssd(z, xbc, dt, hist, hbig0, convw, convb, dtb, alog, dskip, ng, bb, out_dtype):
    b, l, _ = z.shape
    rows = min(l, SSD_CHUNK)
    nc = l // rows
    seq = lambda i, c: (i, c, 0)
    fixed3 = lambda i, c: (i, 0, 0)
    par = lambda i, c: (0, 0)
    return pl.pallas_call(
        functools.partial(_ssd_kernel, rows=rows, bb=bb),
        grid=(b // bb, nc),
        in_specs=[pl.BlockSpec((bb, rows, SSD_DIM), seq),
                  pl.BlockSpec((bb, rows, SSD_CONV_DIM), seq),
                  pl.BlockSpec((bb, rows, LANES), seq),
                  pl.BlockSpec((bb, CONV_ROWS, SSD_CONV_DIM), fixed3),
                  pl.BlockSpec((bb, SSD_DIM, LANES), fixed3),
                  pl.BlockSpec((SSD_CONV, SSD_CONV_DIM), par),
                  pl.BlockSpec((1, SSD_CONV_DIM), par),
                  pl.BlockSpec((1, LANES), par),
                  pl.BlockSpec((1, LANES), par),
                  pl.BlockSpec((1, SSD_DIM), par),
                  pl.BlockSpec((1, SSD_DIM), par)],
        out_specs=[pl.BlockSpec((bb, rows, SSD_DIM), seq),
                   pl.BlockSpec((bb, CONV_ROWS, SSD_CONV_DIM), fixed3),
                   pl.BlockSpec((bb, SSD_DIM, SSD_STATE), fixed3)],
        out_shape=[jax.ShapeDtypeStruct((b, l, SSD_DIM), out_dtype),
                   jax.ShapeDtypeStruct((b, CONV_ROWS, SSD_CONV_DIM), F32),
                   jax.ShapeDtypeStruct((b, SSD_DIM, SSD_STATE), F32)],
        scratch_shapes=[pltpu.VMEM((bb, CONV_ROWS + SSD_CHUNK, SSD_CONV_DIM), F32),
                        pltpu.VMEM((bb, SSD_DIM, LANES), F32)],
        compiler_params=_cparams(2),
        name="ssd_mix",
    )(z, xbc, dt, hist, hbig0, convw, convb, dtb, alog, dskip, ng)


Q_NOPE_W = MLA_HEADS * MLA_NOPE
Q_PE_W = MLA_HEADS * LANES


def _mla_prep_kernel(cq_ref, ckv_ref, kpe_ref, kper_ref, cos_ref, sin_ref, qg_ref, kg_ref, wq_ref, wuk_ref,
                     q_ref, k_ref, ckvn_ref, kr_ref):
    hq = _rms(cq_ref[...], qg_ref[...]).astype(BF)
    q_nope = _dot(hq, wq_ref[:, :Q_NOPE_W])
    q_pe = _dot(hq, wq_ref[:, Q_NOPE_W:Q_NOPE_W + Q_PE_W])
    q_pe_rot = _dot(hq, wq_ref[:, Q_NOPE_W + Q_PE_W:])
    q_lat = _dot(q_nope.astype(BF), wuk_ref[...])
    cos = cos_ref[...]
    sin = sin_ref[...]
    for h in range(MLA_HEADS):
        sl = slice(h * LANES, (h + 1) * LANES)
        q_ref[h, :, 0:LANES] = q_lat[:, sl].astype(q_ref.dtype)
        q_ref[h, :, LANES:QK_WIDTH] = (q_pe[:, sl] * cos + q_pe_rot[:, sl] * sin).astype(q_ref.dtype)
    ckvn = _rms(ckv_ref[...], kg_ref[...])
    ckvn_ref[...] = ckvn
    kr = kpe_ref[...] * cos + kper_ref[...] * sin
    kr_ref[...] = kr[:, :MLA_ROPE]
    k_ref[:, 0:LANES] = ckvn.astype(k_ref.dtype)
    k_ref[:, LANES:QK_WIDTH] = kr.astype(k_ref.dtype)


def _mla_prep(cq, ckv, kpe, kper, cos, sin, qg, kg, wq, wuk, qk_dtype):
    n = cq.shape[0]
    tm = _row_tile(n, 512)
    ntab = cos.shape[0] // tm
    rowblk = lambda i: (i, 0)
    tab = lambda i: (i % ntab, 0)
    par = lambda i: (0, 0)
    return pl.pallas_call(
        _mla_prep_kernel,
        grid=(n // tm,),
        in_specs=[pl.BlockSpec((tm, MLA_Q_LORA), rowblk),
                  pl.BlockSpec((tm, MLA_KV_LORA), rowblk),
                  pl.BlockSpec((tm, LANES), rowblk),
                  pl.BlockSpec((tm, LANES), rowblk),
                  pl.BlockSpec((tm, LANES), tab),
                  pl.BlockSpec((tm, LANES), tab),
                  pl.BlockSpec((1, MLA_Q_LORA), par),
                  pl.BlockSpec((1, MLA_KV_LORA), par),
                  pl.BlockSpec(wq.shape, par),
                  pl.BlockSpec(wuk.shape, par)],
        out_specs=[pl.BlockSpec((MLA_HEADS, tm, QK_WIDTH), lambda i: (0, i, 0)),
                   pl.BlockSpec((tm, QK_WIDTH), rowblk),
                   pl.BlockSpec((tm, MLA_KV_LORA), rowblk),
                   pl.BlockSpec((tm, MLA_ROPE), rowblk)],
        out_shape=[jax.ShapeDtypeStruct((MLA_HEADS, n, QK_WIDTH), qk_dtype),
                   jax.ShapeDtypeStruct((n, QK_WIDTH), qk_dtype),
                   jax.ShapeDtypeStruct((n, MLA_KV_LORA), F32),
                   jax.ShapeDtypeStruct((n, MLA_ROPE), F32)],
        compiler_params=_cparams(1),
        name="mla_prep",
    )(cq, ckv, kpe, kper, cos, sin, qg, kg, wq, wuk)


def _value_up_norm(o_lat, rows, wuv_ref, g_ref):
    out = jnp.zeros((rows, MLA_DIM), F32)
    for h in range(MLA_HEADS):
        out = out + _dot(o_lat[h * rows:(h + 1) * rows].astype(BF), wuv_ref[h])
    return _rms(out, g_ref[...])


def _attn_prompt_kernel(q_ref, k_ref, wuv_ref, g_ref, o_ref, m_sc, l_sc, acc_sc, *, blk):
    i = pl.program_id(1)
    rows = MLA_HEADS * blk
    q = q_ref[...].reshape(rows, QK_WIDTH)
    m_sc[...] = jnp.full((rows, 1), -jnp.inf, F32)
    l_sc[...] = jnp.zeros((rows, 1), F32)
    acc_sc[...] = jnp.zeros((rows, MLA_KV_LORA), F32)

    def step(j, masked):
        kb = k_ref[pl.ds(pl.multiple_of(j * blk, blk), blk), :]
        s = _dot_nt(q, kb) * MLA_SCALE
        if masked:
            t_q = lax.broadcasted_iota(jnp.int32, (1, blk, blk), 1)
            t_k = lax.broadcasted_iota(jnp.int32, (1, blk, blk), 2)
            s = jnp.where(t_k <= t_q, s.reshape(MLA_HEADS, blk, blk), -jnp.inf).reshape(rows, blk)
        m_old = m_sc[...]
        m_new = jnp.maximum(m_old, jnp.max(s, axis=1, keepdims=True))
        alpha = jnp.exp(m_old - m_new)
        p = jnp.exp(s - m_new)
        l_sc[...] = alpha * l_sc[...] + jnp.sum(p, axis=1, keepdims=True)
        acc_sc[...] = alpha * acc_sc[...] + _dot(p.astype(BF), kb[:, :MLA_KV_LORA])
        m_sc[...] = m_new

    def body(j, carry):
        step(j, False)
        return carry

    lax.fori_loop(0, i, body, 0)
    step(i, True)
    o_lat = acc_sc[...] / l_sc[...]
    o_ref[...] = _value_up_norm(o_lat, blk, wuv_ref, g_ref).astype(o_ref.dtype)


def _attn_prompt(q, k, wuv, g, batch, out_dtype):
    n = k.shape[0]
    l = n // batch
    blk = _row_tile(l, 256)
    nq = l // blk
    rows = MLA_HEADS * blk
    return pl.pallas_call(
        functools.partial(_attn_prompt_kernel, blk=blk),
        grid=(batch, nq),
        in_specs=[pl.BlockSpec((MLA_HEADS, blk, QK_WIDTH), lambda b, i: (0, b * nq + i, 0)),
                  pl.BlockSpec((l, QK_WIDTH), lambda b, i: (b, 0)),
                  pl.BlockSpec(wuv.shape, lambda b, i: (0, 0, 0)),
                  pl.BlockSpec((1, MLA_DIM), lambda b, i: (0, 0))],
        out_specs=pl.BlockSpec((blk, MLA_DIM), lambda b, i: (b * nq + i, 0)),
        out_shape=jax.ShapeDtypeStruct((n, MLA_DIM), out_dtype),
        scratch_shapes=[pltpu.VMEM((rows, 1), F32), pltpu.VMEM((rows, 1), F32),
                        pltpu.VMEM((rows, MLA_KV_LORA), F32)],
        compiler_params=_cparams(2),
        name="attn_prompt",
    )(q, k, wuv, g)


def _attn_sample_kernel(pt_ref, q_ref, knew_ref, ckv_hbm, kr_hbm, wuv_ref, g_ref, o_ref,
                        kbuf, rbuf, sems, *, layer, n_pages, t_new):
    b = pl.program_id(0)
    nb = pl.num_programs(0)
    past = n_pages * PAGE_SIZE

    def page_copies(seq, slot, page_of):
        for j in range(n_pages):
            pg = page_of(seq, j)
            rows = pl.ds(j * PAGE_SIZE, PAGE_SIZE)
            yield pltpu.make_async_copy(ckv_hbm.at[layer, pg], kbuf.at[slot, rows, :], sems.at[0, slot])
            yield pltpu.make_async_copy(kr_hbm.at[layer, pg], rbuf.at[slot, rows, :], sems.at[1, slot])

    def start_fetch(seq, slot):
        for cp in page_copies(seq, slot, lambda s, j: pt_ref[s, j]):
            cp.start()

    def wait_fetch(slot):
        for cp in page_copies(0, slot, lambda s, j: 0):
            cp.wait()

    @pl.when(b == 0)
    def _():
        start_fetch(0, 0)

    @pl.when(b + 1 < nb)
    def _():
        start_fetch(b + 1, (b + 1) % 2)

    slot = b % 2
    wait_fetch(slot)

    rows = MLA_HEADS * t_new
    q = q_ref[...].reshape(rows, QK_WIDTH).astype(BF)
    kf = kbuf[slot].astype(BF)
    rf = rbuf[slot].astype(BF)
    s = (_dot_nt(q[:, :MLA_KV_LORA], kf)
         + _dot_nt(q[:, MLA_KV_LORA:MLA_KV_LORA + MLA_ROPE], rf)) * MLA_SCALE
    kn = jnp.concatenate([knew_ref[...], jnp.zeros((LANES - t_new, QK_WIDTH), knew_ref.dtype)],
                         axis=0).astype(BF)
    sn = _dot_nt(q, kn) * MLA_SCALE
    t_q = lax.broadcasted_iota(jnp.int32, (1, t_new, LANES), 1)
    t_k = lax.broadcasted_iota(jnp.int32, (1, t_new, LANES), 2)
    sn = jnp.where(t_k <= t_q, sn.reshape(MLA_HEADS, t_new, LANES), -jnp.inf).reshape(rows, LANES)
    m = jnp.maximum(jnp.max(s, axis=1, keepdims=True), jnp.max(sn, axis=1, keepdims=True))
    p = jnp.exp(s - m)
    pn = jnp.exp(sn - m)
    denom = jnp.sum(p, axis=1, keepdims=True) + jnp.sum(pn, axis=1, keepdims=True)
    o_lat = (_dot(p.astype(BF), kf) + _dot(pn.astype(BF), kn[:, :MLA_KV_LORA])) / denom
    o_ref[...] = _value_up_norm(o_lat, t_new, wuv_ref, g_ref).astype(o_ref.dtype)


def _attn_sample(page_table, q, knew, cache_kv, cache_kr, wuv, g, layer, out_dtype):
    nseq, n_pages = page_table.shape
    n = knew.shape[0]
    t_new = n // nseq
    past = n_pages * PAGE_SIZE
    grid_spec = pltpu.PrefetchScalarGridSpec(
        num_scalar_prefetch=1,
        grid=(nseq,),
        in_specs=[pl.BlockSpec((MLA_HEADS, t_new, QK_WIDTH), lambda b, pt: (0, b, 0)),
                  pl.BlockSpec((t_new, QK_WIDTH), lambda b, pt: (b, 0)),
                  pl.BlockSpec(memory_space=pl.ANY),
                  pl.BlockSpec(memory_space=pl.ANY),
                  pl.BlockSpec(wuv.shape, lambda b, pt: (0, 0, 0)),
                  pl.BlockSpec((1, MLA_DIM), lambda b, pt: (0, 0))],
        out_specs=pl.BlockSpec((t_new, MLA_DIM), lambda b, pt: (b, 0)),
        scratch_shapes=[pltpu.VMEM((2, past, MLA_KV_LORA), F32),
                        pltpu.VMEM((2, past, MLA_ROPE), F32),
                        pltpu.SemaphoreType.DMA((2, 2))],
    )
    return pl.pallas_call(
        functools.partial(_attn_sample_kernel, layer=layer, n_pages=n_pages, t_new=t_new),
        grid_spec=grid_spec,
        out_shape=jax.ShapeDtypeStruct((n, MLA_DIM), out_dtype),
        compiler_params=_cparams(1),
        name="attn_sample",
    )(page_table, q, knew, cache_kv, cache_kr, wuv, g)


def _out_ffn_kernel(x_ref, pool_ref, ssd_ref, mla_ref, wout_ref, g_ref, wup_ref, wdn_ref, o_ref,
                    xn_sc, hn_sc, acc_sc):
    k = pl.program_id(1)

    @pl.when(k == 0)
    def _():
        y = (_dot(pool_ref[...].astype(BF), wout_ref[0:POOL_DIM])
             + _dot(ssd_ref[...].astype(BF), wout_ref[POOL_DIM:POOL_DIM + SSD_DIM])
             + _dot(mla_ref[...].astype(BF), wout_ref[POOL_DIM + SSD_DIM:]))
        xn = x_ref[...] + y
        xn_sc[...] = xn
        hn_sc[...] = _rms(xn, g_ref[...]).astype(BF)
        acc_sc[...] = jnp.zeros_like(acc_sc)

    a = jnp.maximum(_dot(hn_sc[...], wup_ref[...]), 0.0)
    acc_sc[...] += _dot((a * a).astype(BF), wdn_ref[...])

    @pl.when(k == pl.num_programs(1) - 1)
    def _():
        o_ref[...] = xn_sc[...] + acc_sc[...]


def _out_ffn(x, pool, ssd, mla, wout, g, wup, wdn):
    n = x.shape[0]
    tm = _row_tile(n, 512)
    tf = 1024
    rowblk = lambda i, k: (i, 0)
    par = lambda i, k: (0, 0)
    return pl.pallas_call(
        _out_ffn_kernel,
        grid=(n // tm, FFN_DIM // tf),
        in_specs=[pl.BlockSpec((tm, D_MODEL), rowblk),
                  pl.BlockSpec((tm, POOL_DIM), rowblk),
                  pl.BlockSpec((tm, SSD_DIM), rowblk),
                  pl.BlockSpec((tm, MLA_DIM), rowblk),
                  pl.BlockSpec((D_MODEL, D_MODEL), par),
                  pl.BlockSpec((1, D_MODEL), par),
                  pl.BlockSpec((D_MODEL, tf), lambda i, k: (0, k)),
                  pl.BlockSpec((tf, D_MODEL), lambda i, k: (k, 0))],
        out_specs=pl.BlockSpec((tm, D_MODEL), rowblk),
        out_shape=jax.ShapeDtypeStruct((n, D_MODEL), F32),
        scratch_shapes=[pltpu.VMEM((tm, D_MODEL), F32), pltpu.VMEM((tm, D_MODEL), BF),
                        pltpu.VMEM((tm, D_MODEL), F32)],
        compiler_params=_cparams(2),
        name="out_ffn",
    )(x, pool, ssd, mla, wout, g, wup, wdn)


def _final_norm_kernel(x_ref, g_ref, o_ref):
    o_ref[...] = _rms(x_ref[...], g_ref[...])


def _final_norm(x, g):
    n = x.shape[0]
    tm = _row_tile(n, 512)
    return pl.pallas_call(
        _final_norm_kernel,
        grid=(n // tm,),
        in_specs=[pl.BlockSpec((tm, D_MODEL), lambda i: (i, 0)),
                  pl.BlockSpec((1, D_MODEL), lambda i: (0, 0))],
        out_specs=pl.BlockSpec((tm, D_MODEL), lambda i: (i, 0)),
        out_shape=jax.ShapeDtypeStruct((n, D_MODEL), F32),
        compiler_params=_cparams(1),
        name="final_norm",
    )(x, g)


def _pad_cols(a, width):
    return jnp.pad(a, [(0, 0)] * (a.ndim - 1) + [(0, width - a.shape[-1])])


def _rot_half_cols(w):
    half = MLA_ROPE // 2
    return jnp.concatenate([-w[..., half:], w[..., :half]], axis=-1)


def _prep_weights(w_in, pool_w, w_uq, w_uk, w_uv, w_out, w_up, w_down):
    depth = w_in.shape[0]
    parts, start = [], 0
    for wd in IN_WIDTHS:
        parts.append(w_in[..., start:start + wd])
        start += wd
    u, z, xbc, dt, cq, ckv, kpe = parts
    w_in_p = jnp.concatenate([u, z, xbc, cq, ckv, _pad_cols(kpe, LANES), _pad_cols(_rot_half_cols(kpe), LANES),
                              _pad_cols(dt, LANES)], axis=-1).astype(BF)
    eye4 = jnp.eye(len(POOL_WINDOWS), dtype=F32)
    pool_bd = (pool_w[:, :, :, None, :] * eye4[None, :, None, :, None]).reshape(depth, POOL_DIM, POOL_DIM).astype(BF)
    wq = w_uq.reshape(depth, MLA_Q_LORA, MLA_HEADS, MLA_NOPE + MLA_ROPE)
    wq_nope = wq[..., :MLA_NOPE].reshape(depth, MLA_Q_LORA, Q_NOPE_W)
    wq_pe = wq[..., MLA_NOPE:]
    wq_p = jnp.concatenate([wq_nope,
                            _pad_cols(wq_pe, LANES).reshape(depth, MLA_Q_LORA, Q_PE_W),
                            _pad_cols(_rot_half_cols(wq_pe), LANES).reshape(depth, MLA_Q_LORA, Q_PE_W)],
                           axis=-1).astype(BF)
    eye6 = jnp.eye(MLA_HEADS, dtype=F32)
    uk = jnp.transpose(w_uk, (0, 2, 3, 1))
    wuk_bd = (uk[:, :, :, None, :] * eye6[None, :, None, :, None]).reshape(
        depth, Q_NOPE_W, MLA_HEADS * MLA_KV_LORA).astype(BF)
    uv = jnp.transpose(w_uv, (0, 2, 1, 3))
    wuv_pad = (uv[:, :, :, None, :] * eye6[None, :, None, :, None]).reshape(
        depth, MLA_HEADS, MLA_KV_LORA, MLA_DIM).astype(BF)
    return w_in_p, pool_bd, wq_p, wuk_bd, wuv_pad, w_out.astype(BF), w_up.astype(BF), w_down.astype(BF)


def _rope_tables(pos, reps):
    half = MLA_ROPE // 2
    inv = ROPE_THETA ** (-jnp.arange(half, dtype=F32) * (2.0 / MLA_ROPE))
    ang = pos.astype(F32)[:, None] * inv[None, :]
    cos = jnp.cos(ang)
    sin = jnp.sin(ang)
    cos = _pad_cols(jnp.concatenate([cos, cos], axis=-1), LANES)
    sin = _pad_cols(jnp.concatenate([sin, sin], axis=-1), LANES)
    return jnp.tile(cos, (reps, 1)), jnp.tile(sin, (reps, 1))


def _state_to_rows(h):
    return _pad_cols(h.reshape(h.shape[0], SSD_DIM, SSD_STATE), LANES)


def _group_layer(x, batch, lw, pos0, pool_hist, conv_hist, ssm_h0, tables, attend, bb, act_dtype):
    n = x.shape[0]
    l = n // batch
    u, z, xbc, cq, ckv, kpe, kper, dt = _in_proj(x, lw["norm_mix_g"], lw["w_in"])

    pool_out, pool_tail = _pool(u.reshape(batch, l, POOL_DIM), pool_hist, lw["pool_bd"], lw["pool_scale"],
                                pos0, act_dtype)
    ssd_out, conv_tail, hbig = _ssd(z.reshape(batch, l, SSD_DIM), xbc.reshape(batch, l, SSD_CONV_DIM),
                                    dt.reshape(batch, l, LANES), conv_hist, ssm_h0,
                                    lw["conv_w"], lw["conv_b"], lw["dt_bias"], lw["a_log"], lw["d_skip"],
                                    lw["ssd_norm_g"], bb, act_dtype)
    q, kcat, ckv_n, kpe_r = _mla_prep(cq, ckv, kpe, kper, tables[0], tables[1], lw["q_norm_g"], lw["kv_norm_g"],
                                      lw["wq"], lw["wuk_bd"], act_dtype)
    mla_out = attend(q, kcat)
    x_new = _out_ffn(x, pool_out.reshape(n, POOL_DIM), ssd_out.reshape(n, SSD_DIM), mla_out,
                     lw["w_out"], lw["norm_ffn_g"], lw["w_up"], lw["w_down"])
    new_pool = pool_tail[:, HIST_ROWS - POOL_HIST:]
    new_conv = conv_tail[:, CONV_ROWS - (SSD_CONV - 1):]
    new_ssm = hbig.reshape(batch, SSD_HEADS, SSD_HEADDIM, SSD_STATE)
    return (x_new, ckv_n.reshape(batch, l, MLA_KV_LORA), kpe_r.reshape(batch, l, MLA_ROPE),
            new_pool, new_conv, new_ssm)


def kernel(x_prompt, x_sample, cache_kv_latent, cache_k_rope, state_pool, state_conv, state_ssm, page_table,
           norm_mix_g, w_in, pool_w, pool_scale, conv_w, conv_b, dt_bias, a_log, d_skip, ssd_norm_g, q_norm_g,
           w_uq, kv_norm_g, w_uk, w_uv, mla_out_g, w_out, norm_ffn_g, w_up, w_down, final_norm_g):
    depth = w_in.shape[0]
    bp, seq, _ = x_prompt.shape
    db, dec_seq, _ = x_sample.shape
    past_len = page_table.shape[1] * PAGE_SIZE

    w_in_p, pool_bd, wq_p, wuk_bd, wuv_pad, w_out_b, w_up_b, w_down_b = _prep_weights(
        w_in, pool_w, w_uq, w_uk, w_uv, w_out, w_up, w_down)
    tables_p = _rope_tables(jnp.arange(seq, dtype=jnp.int32), 1)
    tile_s = _row_tile(db * dec_seq, 512)
    tables_s = _rope_tables(past_len + jnp.arange(dec_seq, dtype=jnp.int32), tile_s // dec_seq)

    xp = x_prompt.reshape(bp * seq, D_MODEL)
    xs = x_sample.reshape(db * dec_seq, D_MODEL)
    zero_pool = jnp.zeros((bp, HIST_ROWS, POOL_DIM), F32)
    zero_conv = jnp.zeros((bp, CONV_ROWS, SSD_CONV_DIM), F32)
    zero_ssm = jnp.zeros((bp, SSD_DIM, LANES), F32)
    outs_p, outs_s = [], []
    for l in range(depth):
        lw = {
            "norm_mix_g": norm_mix_g[l][None], "w_in": w_in_p[l], "pool_bd": pool_bd[l],
            "pool_scale": pool_scale[l][None], "conv_w": conv_w[l], "conv_b": conv_b[l][None],
            "dt_bias": _pad_cols(dt_bias[l][None], LANES), "a_log": _pad_cols(a_log[l][None], LANES),
            "d_skip": jnp.repeat(d_skip[l], SSD_HEADDIM)[None], "ssd_norm_g": ssd_norm_g[l][None],
            "q_norm_g": q_norm_g[l][None], "kv_norm_g": kv_norm_g[l][None], "wq": wq_p[l], "wuk_bd": wuk_bd[l],
            "w_out": w_out_b[l], "norm_ffn_g": norm_ffn_g[l][None], "w_up": w_up_b[l], "w_down": w_down_b[l],
        }
        wuv_l = wuv_pad[l]
        og = mla_out_g[l][None]

        attend_p = lambda q, k: _attn_prompt(q, k, wuv_l, og, bp, BF)
        res = _group_layer(xp, bp, lw, 0, zero_pool, zero_conv, zero_ssm, tables_p, attend_p, 1, BF)
        xp = res[0]
        outs_p.append(res[1:])

        attend_s = lambda q, k: _attn_sample(page_table, q, k, cache_kv_latent, cache_k_rope, wuv_l, og, l, F32)
        pool_hist = jnp.pad(state_pool[l], ((0, 0), (HIST_ROWS - POOL_HIST, 0), (0, 0)))
        conv_hist = jnp.pad(state_conv[l], ((0, 0), (CONV_ROWS - (SSD_CONV - 1), 0), (0, 0)))
        res = _group_layer(xs, db, lw, past_len, pool_hist, conv_hist, _state_to_rows(state_ssm[l]),
                           tables_s, attend_s, 8, F32)
        xs = res[0]
        outs_s.append(res[1:])

    y_prompt = _final_norm(xp, final_norm_g[None]).reshape(bp, seq, D_MODEL)
    y_sample = _final_norm(xs, final_norm_g[None]).reshape(db, dec_seq, D_MODEL)
    stack = lambda outs, k: jnp.stack([o[k] for o in outs])
    return (y_prompt, y_sample,
            stack(outs_p, 0), stack(outs_p, 1), stack(outs_p, 2), stack(outs_p, 3), stack(outs_p, 4),
            stack(outs_s, 0), stack(outs_s, 1), stack(outs_s, 2), stack(outs_s, 3), stack(outs_s, 4))
```

```python
import functools

import jax
import jax.numpy as jnp
from jax import lax
from jax.experimental import pallas as pl
from jax.experimental.pallas import tpu as pltpu

D_MODEL = 1024
POOL_WINDOWS = (2, 4, 8, 16)
POOL_DIM = 256
POOL_GROUP = 64
POOL_HIST = 15
SSD_HEADDIM = 64
SSD_HEADS = 6
SSD_DIM = 384
SSD_GROUPS = 2
SSD_STATE = 64
SSD_CONV = 4
SSD_CONV_DIM = 640
SSD_CHUNK = 128
MLA_HEADS = 6
MLA_NOPE = 64
MLA_ROPE = 32
MLA_V = 64
MLA_Q_LORA = 256
MLA_KV_LORA = 128
MLA_DIM = 384
MLA_SCALE = (MLA_NOPE + MLA_ROPE) ** -0.5
ROPE_THETA = 10000.0
FFN_DIM = 4096
NORM_EPS = 1e-6
PAGE_SIZE = 128
IN_WIDTHS = (POOL_DIM, SSD_DIM, SSD_CONV_DIM, SSD_HEADS, MLA_Q_LORA, MLA_KV_LORA, MLA_ROPE)

LANES = 128
SUBLANES = 8
HIST_ROWS = 16
CONV_ROWS = 8
QK_WIDTH = 2 * LANES
VMEM_LIMIT = 48 * 1024 * 1024
ATTN_BQ = 256
ATTN_BK = 512
ATTN_HEAD_UNROLL = 6
LOG2E = 1.4426950408889634

BF = jnp.bfloat16
F32 = jnp.float32


def _cparams(n_axes):
    return pltpu.CompilerParams(dimension_semantics=("arbitrary",) * n_axes,
                                vmem_limit_bytes=VMEM_LIMIT)


def _rms(x, g):
    return x * lax.rsqrt(jnp.mean(x * x, axis=-1, keepdims=True) + NORM_EPS) * g


def _dot(a, b):
    return jnp.dot(a, b, preferred_element_type=F32)


def _dot_nt(a, b):
    return lax.dot_general(a, b, (((1,), (1,)), ((), ())), preferred_element_type=F32)


def _row_tile(n, target):
    t = min(n, target)
    while n % t:
        t //= 2
    return t


IN_OUT_WIDTHS = (POOL_DIM, SSD_DIM, SSD_CONV_DIM, MLA_Q_LORA, MLA_KV_LORA, LANES, LANES, LANES)


def _in_proj_kernel(x_ref, g_ref, w_ref, *out_refs):
    h = _rms(x_ref[...], g_ref[...]).astype(BF)
    off = 0
    for o_ref in out_refs:
        width = o_ref.shape[-1]
        o_ref[...] = _dot(h, w_ref[:, off:off + width])
        off += width


def _in_proj(x, g, w):
    n = x.shape[0]
    tm = _row_tile(n, 512)
    wtot = w.shape[1]
    return pl.pallas_call(
        _in_proj_kernel,
        grid=(n // tm,),
        in_specs=[pl.BlockSpec((tm, D_MODEL), lambda i: (i, 0)),
                  pl.BlockSpec((1, D_MODEL), lambda i: (0, 0)),
                  pl.BlockSpec((D_MODEL, wtot), lambda i: (0, 0))],
        out_specs=[pl.BlockSpec((tm, wd), lambda i: (i, 0)) for wd in IN_OUT_WIDTHS],
        out_shape=[jax.ShapeDtypeStruct((n, wd), F32) for wd in IN_OUT_WIDTHS],
        compiler_params=_cparams(1),
        name="in_proj",
    )(x, g, w)


def _pool_kernel(u_ref, hist_ref, w_ref, scale_ref, out_ref, newhist_ref, carry_sc, *, pos0, tile):
    t = pl.program_id(1)

    @pl.when(t == 0)
    def _():
        carry_sc[...] = hist_ref[0]

    u = u_ref[0]
    ext = jnp.concatenate([carry_sc[...], u], axis=0)
    s2 = ext + pltpu.roll(ext, 1, 0)
    s4 = s2 + pltpu.roll(s2, 2, 0)
    s8 = s4 + pltpu.roll(s4, 4, 0)
    s16 = s8 + pltpu.roll(s8, 8, 0)
    lane = lax.broadcasted_iota(jnp.int32, (1, POOL_DIM), 1)
    g0, g1, g2 = lane < POOL_GROUP, lane < 2 * POOL_GROUP, lane < 3 * POOL_GROUP
    wsum = jnp.where(g0, s2, jnp.where(g1, s4, jnp.where(g2, s8, s16)))[HIST_ROWS:]
    width = jnp.where(g0, 2, jnp.where(g1, 4, jnp.where(g2, 8, 16)))
    pos = pos0 + t * tile + lax.broadcasted_iota(jnp.int32, (tile, 1), 0)
    cnt = jnp.minimum(width, pos + 1).astype(F32)
    m = wsum / cnt - u
    out = _dot(m.astype(BF), w_ref[...]) * scale_ref[...]
    out_ref[0] = out.astype(out_ref.dtype)
    last = ext[tile:tile + HIST_ROWS]
    carry_sc[...] = last
    newhist_ref[0] = last


def _pool(u, hist, wbd, scale, pos0, out_dtype):
    b, l, _ = u.shape
    tile = _row_tile(l, 256)
    return pl.pallas_call(
        functools.partial(_pool_kernel, pos0=pos0, tile=tile),
        grid=(b, l // tile),
        in_specs=[pl.BlockSpec((1, tile, POOL_DIM), lambda i, t: (i, t, 0)),
                  pl.BlockSpec((1, HIST_ROWS, POOL_DIM), lambda i, t: (i, 0, 0)),
                  pl.BlockSpec((POOL_DIM, POOL_DIM), lambda i, t: (0, 0)),
                  pl.BlockSpec((1, POOL_DIM), lambda i, t: (0, 0))],
        out_specs=[pl.BlockSpec((1, tile, POOL_DIM), lambda i, t: (i, t, 0)),
                   pl.BlockSpec((1, HIST_ROWS, POOL_DIM), lambda i, t: (i, 0, 0))],
        out_shape=[jax.ShapeDtypeStruct((b, l, POOL_DIM), out_dtype),
                   jax.ShapeDtypeStruct((b, HIST_ROWS, POOL_DIM), F32)],
        scratch_shapes=[pltpu.VMEM((HIST_ROWS, POOL_DIM), F32)],
        compiler_params=_cparams(2),
        name="pool_mix",
    )(u, hist, wbd, scale)


def _expand_heads(x):
    q = x.shape[0]
    col = lax.broadcasted_iota(jnp.int32, (1, SSD_DIM), 1)
    out = jnp.zeros((q, SSD_DIM), F32)
    for h in range(SSD_HEADS):
        sel = (col >= h * SSD_HEADDIM) & (col < (h + 1) * SSD_HEADDIM)
        out = jnp.where(sel, jnp.broadcast_to(x[:, h:h + 1], (q, SSD_DIM)), out)
    return out


def _ssd_kernel(z_ref, xbc_ref, dt_ref, hist_ref, h0_ref, convw_ref, convb_ref, dtb_ref, alog_ref,
                dskip_ref, ng_ref, y_ref, newconv_ref, ht_ref, ext_sc, hbig_sc, *, rows, bb):
    c = pl.program_id(1)
    q = SSD_CHUNK
    pad = q - rows

    def padded(x):
        if pad == 0:
            return x
        return jnp.concatenate([x, jnp.zeros((pad, x.shape[1]), x.dtype)], axis=0)

    row = lax.broadcasted_iota(jnp.int32, (q, 1), 0)
    col_t = lax.broadcasted_iota(jnp.int32, (1, q), 1)
    causal = row >= col_t
    lane = lax.broadcasted_iota(jnp.int32, (1, LANES), 1)
    col384 = lax.broadcasted_iota(jnp.int32, (1, SSD_DIM), 1)
    row384 = lax.broadcasted_iota(jnp.int32, (SSD_DIM, 1), 0)
    upper_rows = row384 >= SSD_DIM // SSD_GROUPS
    state_valid = jnp.where(upper_rows, 1, 0) == jnp.where(lane >= SSD_STATE, 1, 0)
    a_neg = -jnp.exp(alog_ref[...])

    def per_seq(i, carry):
        @pl.when(c == 0)
        def _():
            ext_sc[i, 0:CONV_ROWS, :] = hist_ref[i]
            h0 = h0_ref[i]
            hbig_sc[i] = jnp.where(upper_rows, pltpu.roll(h0, SSD_STATE, 1), h0)

        ext_sc[i, CONV_ROWS:CONV_ROWS + q, :] = padded(xbc_ref[i])
        conv = convb_ref[...]
        for k in range(SSD_CONV):
            start = CONV_ROWS - (SSD_CONV - 1) + k
            conv = conv + convw_ref[k:k + 1, :] * ext_sc[i, start:start + q, :]
        tail = ext_sc[i, rows:rows + CONV_ROWS, :]
        newconv_ref[i] = tail
        ext_sc[i, 0:CONV_ROWS, :] = tail

        act = conv * jax.nn.sigmoid(conv)
        xs = act[:, :SSD_DIM]
        bmat = act[:, SSD_DIM:SSD_DIM + LANES]
        cmat = act[:, SSD_DIM + LANES:]

        dtr = padded(dt_ref[i]) + dtb_ref[...]
        dt = jnp.maximum(dtr, 0.0) + jnp.log(1.0 + jnp.exp(-jnp.abs(dtr)))
        dt = jnp.where(row < rows, dt, 0.0)
        a_cs = dt * a_neg
        k = 1
        while k < q:
            a_cs = a_cs + jnp.where(row >= k, pltpu.roll(a_cs, k, 0), 0.0)
            k *= 2
        a_cs_t = a_cs.T
        a_last = a_cs[q - 1:q, :]
        decay_to_end = jnp.exp(a_last - a_cs)
        decay_from_start = jnp.exp(a_cs)
        chunk_decay = jnp.exp(a_last)

        dt_cols = _expand_heads(dt)
        xdt = xs * dt_cols
        xdec = xs * _expand_heads(dt * decay_to_end)

        bmat_b = bmat.astype(BF)
        cb = [_dot_nt(jnp.where((lane >= g * SSD_STATE) & (lane < (g + 1) * SSD_STATE), cmat, 0.0).astype(BF),
                      bmat_b) for g in range(SSD_GROUPS)]
        y = jnp.zeros((q, SSD_DIM), F32)
        for h in range(SSD_HEADS):
            seg = a_cs[:, h:h + 1] - a_cs_t[h:h + 1, :]
            lmat = jnp.exp(jnp.where(causal, seg, -jnp.inf))
            m = cb[h // (SSD_HEADS // SSD_GROUPS)] * lmat
            sel = (col384 >= h * SSD_HEADDIM) & (col384 < (h + 1) * SSD_HEADDIM)
            y = y + _dot(m.astype(BF), jnp.where(sel, xdt, 0.0).astype(BF))

        hbig = hbig_sc[i]
        y_off = _dot_nt(cmat.astype(BF), hbig.astype(BF)) * _expand_heads(decay_from_start)
        y = y + y_off + dskip_ref[...] * xs
        zf = padded(z_ref[i])
        gated = y * (zf * jax.nn.sigmoid(zf))
        y_ref[i] = _rms(gated, ng_ref[...])[:rows].astype(y_ref.dtype)

        new_states = _dot(xdec.T.astype(BF), bmat_b)
        cd_rows = _expand_heads(jnp.broadcast_to(chunk_decay, (q, LANES))).T[:, :LANES]
        hnew = hbig * cd_rows + jnp.where(state_valid, new_states, 0.0)
        hbig_sc[i] = hnew
        ht_ref[i] = jnp.where(upper_rows, pltpu.roll(hnew, SSD_STATE, 1), hnew)[:, :SSD_STATE]
        return carry

    lax.fori_loop(0, bb, per_seq, 0)


def _ssd(z, xbc, dt, hist, hbig0, convw, convb, dtb, alog, dskip, ng, bb, out_dtype):
    b, l, _ = z.shape
    rows = min(l, SSD_CHUNK)
    nc = l // rows
    seq = lambda i, c: (i, c, 0)
    fixed3 = lambda i, c: (i, 0, 0)
    par = lambda i, c: (0, 0)
    return pl.pallas_call(
        functools.partial(_ssd_kernel, rows=rows, bb=bb),
        grid=(b // bb, nc),
        in_specs=[pl.BlockSpec((bb, rows, SSD_DIM), seq),
                  pl.BlockSpec((bb, rows, SSD_CONV_DIM), seq),
                  pl.BlockSpec((bb, rows, LANES), seq),
                  pl.BlockSpec((bb, CONV_ROWS, SSD_CONV_DIM), fixed3),
                  pl.BlockSpec((bb, SSD_DIM, LANES), fixed3),
                  pl.BlockSpec((SSD_CONV, SSD_CONV_DIM), par),
                  pl.BlockSpec((1, SSD_CONV_DIM), par),
                  pl.BlockSpec((1, LANES), par),
                  pl.BlockSpec((1, LANES), par),
                  pl.BlockSpec((1, SSD_DIM), par),
                  pl.BlockSpec((1, SSD_DIM), par)],
        out_specs=[pl.BlockSpec((bb, rows, SSD_DIM), seq),
                   pl.BlockSpec((bb, CONV_ROWS, SSD_CONV_DIM), fixed3),
                   pl.BlockSpec((bb, SSD_DIM, SSD_STATE), fixed3)],
        out_shape=[jax.ShapeDtypeStruct((b, l, SSD_DIM), out_dtype),
                   jax.ShapeDtypeStruct((b, CONV_ROWS, SSD_CONV_DIM), F32),
                   jax.ShapeDtypeStruct((b, SSD_DIM, SSD_STATE), F32)],
        scratch_shapes=[pltpu.VMEM((bb, CONV_ROWS + SSD_CHUNK, SSD_CONV_DIM), F32),
                        pltpu.VMEM((bb, SSD_DIM, LANES), F32)],
        compiler_params=_cparams(2),
        name="ssd_mix",
    )(z, xbc, dt, hist, hbig0, convw, convb, dtb, alog, dskip, ng)


Q_NOPE_W = MLA_HEADS * MLA_NOPE
Q_PE_W = MLA_HEADS * LANES


def _mla_prep_kernel(cq_ref, ckv_ref, kpe_ref, kper_ref, cos_ref, sin_ref, qg_ref, kg_ref, wq_ref, wuk_ref,
                     q_ref, k_ref, ckvn_ref, kr_ref):
    hq = _rms(cq_ref[...], qg_ref[...]).astype(BF)
    q_nope = _dot(hq, wq_ref[:, :Q_NOPE_W])
    q_pe = _dot(hq, wq_ref[:, Q_NOPE_W:Q_NOPE_W + Q_PE_W])
    q_pe_rot = _dot(hq, wq_ref[:, Q_NOPE_W + Q_PE_W:])
    q_lat = _dot(q_nope.astype(BF), wuk_ref[...])
    cos = cos_ref[...]
    sin = sin_ref[...]
    for h in range(MLA_HEADS):
        sl = slice(h * LANES, (h + 1) * LANES)
        q_ref[h, :, 0:LANES] = q_lat[:, sl].astype(q_ref.dtype)
        q_ref[h, :, LANES:QK_WIDTH] = (q_pe[:, sl] * cos + q_pe_rot[:, sl] * sin).astype(q_ref.dtype)
    ckvn = _rms(ckv_ref[...], kg_ref[...])
    ckvn_ref[...] = ckvn
    kr = kpe_ref[...] * cos + kper_ref[...] * sin
    kr_ref[...] = kr[:, :MLA_ROPE]
    k_ref[:, 0:LANES] = ckvn.astype(k_ref.dtype)
    k_ref[:, LANES:QK_WIDTH] = kr.astype(k_ref.dtype)


def _mla_prep(cq, ckv, kpe, kper, cos, sin, qg, kg, wq, wuk, qk_dtype):
    n = cq.shape[0]
    tm = _row_tile(n, 512)
    ntab = cos.shape[0] // tm
    rowblk = lambda i: (i, 0)
    tab = lambda i: (i % ntab, 0)
    par = lambda i: (0, 0)
    return pl.pallas_call(
        _mla_prep_kernel,
        grid=(n // tm,),
        in_specs=[pl.BlockSpec((tm, MLA_Q_LORA), rowblk),
                  pl.BlockSpec((tm, MLA_KV_LORA), rowblk),
                  pl.BlockSpec((tm, LANES), rowblk),
                  pl.BlockSpec((tm, LANES), rowblk),
                  pl.BlockSpec((tm, LANES), tab),
                  pl.BlockSpec((tm, LANES), tab),
                  pl.BlockSpec((1, MLA_Q_LORA), par),
                  pl.BlockSpec((1, MLA_KV_LORA), par),
                  pl.BlockSpec(wq.shape, par),
                  pl.BlockSpec(wuk.shape, par)],
        out_specs=[pl.BlockSpec((MLA_HEADS, tm, QK_WIDTH), lambda i: (0, i, 0)),
                   pl.BlockSpec((tm, QK_WIDTH), rowblk),
                   pl.BlockSpec((tm, MLA_KV_LORA), rowblk),
                   pl.BlockSpec((tm, MLA_ROPE), rowblk)],
        out_shape=[jax.ShapeDtypeStruct((MLA_HEADS, n, QK_WIDTH), qk_dtype),
                   jax.ShapeDtypeStruct((n, QK_WIDTH), qk_dtype),
                   jax.ShapeDtypeStruct((n, MLA_KV_LORA), F32),
                   jax.ShapeDtypeStruct((n, MLA_ROPE), F32)],
        compiler_params=_cparams(1),
        name="mla_prep",
    )(cq, ckv, kpe, kper, cos, sin, qg, kg, wq, wuk)


def _value_up_norm(o_lat, rows, wuv_ref, g_ref):
    out = jnp.zeros((rows, MLA_DIM), F32)
    for h in range(MLA_HEADS):
        out = out + _dot(o_lat[h * rows:(h + 1) * rows].astype(BF), wuv_ref[h])
    return _rms(out, g_ref[...])


def _attn_prompt_kernel(q_ref, k_ref, wuv_ref, g_ref, o_ref, m_sc, l_sc, acc_sc, *, bq, bk, hu):
    i = pl.program_id(1)
    m_sc[...] = jnp.full(m_sc.shape, -jnp.inf, F32)
    l_sc[...] = jnp.zeros(l_sc.shape, F32)
    acc_sc[...] = jnp.zeros(acc_sc.shape, F32)
    exp2_scale = MLA_SCALE * LOG2E

    def head_step(h, kb, mask):
        s = _dot_nt(q_ref[h], kb)
        if mask is not None:
            s = jnp.where(mask, s, -jnp.inf)
        m_old = m_sc[h]
        m_new = jnp.maximum(m_old, jnp.max(s, axis=1, keepdims=True))
        alpha = jnp.exp2((m_old - m_new) * exp2_scale)
        p = jnp.exp2((s - jnp.concatenate([m_new] * (bk // LANES), axis=1)) * exp2_scale)
        l_sc[h] = alpha * l_sc[h] + jnp.sum(p, axis=1, keepdims=True)
        acc_sc[h] = alpha * acc_sc[h] + _dot(p.astype(BF), kb[:, :MLA_KV_LORA])
        m_sc[h] = m_new

    def kv_step(j, masked):
        start = pl.multiple_of(j * bk, bk)
        mask = None
        if masked:
            t_q = i * bq + lax.broadcasted_iota(jnp.int32, (bq, bk), 0)
            t_k = start + lax.broadcasted_iota(jnp.int32, (bq, bk), 1)
            mask = t_k <= t_q

        def head_group(gi, carry):
            kb = k_ref[pl.ds(start, bk), :]
            for u in range(hu):
                head_step(gi * hu + u, kb, mask)
            return carry

        lax.fori_loop(0, MLA_HEADS // hu, head_group, 0)

    n_full = (i * bq) // bk

    def body(j, carry):
        kv_step(j, False)
        return carry

    lax.fori_loop(0, n_full, body, 0)
    kv_step(n_full, True)
    out = jnp.zeros((bq, MLA_DIM), F32)
    for h in range(MLA_HEADS):
        out = out + _dot((acc_sc[h] / l_sc[h]).astype(BF), wuv_ref[h])
    o_ref[...] = _rms(out, g_ref[...]).astype(o_ref.dtype)


def _attn_prompt(q, k, wuv, g, batch, out_dtype):
    n = k.shape[0]
    l = n // batch
    bq = _row_tile(l, ATTN_BQ)
    bk = _row_tile(l, ATTN_BK)
    nq = l // bq
    stat = pltpu.VMEM((MLA_HEADS, bq, LANES), F32)
    return pl.pallas_call(
        functools.partial(_attn_prompt_kernel, bq=bq, bk=bk, hu=ATTN_HEAD_UNROLL),
        grid=(batch, nq),
        in_specs=[pl.BlockSpec((MLA_HEADS, bq, QK_WIDTH), lambda b, i: (0, b * nq + i, 0)),
                  pl.BlockSpec((l, QK_WIDTH), lambda b, i: (b, 0)),
                  pl.BlockSpec(wuv.shape, lambda b, i: (0, 0, 0)),
                  pl.BlockSpec((1, MLA_DIM), lambda b, i: (0, 0))],
        out_specs=pl.BlockSpec((bq, MLA_DIM), lambda b, i: (b * nq + i, 0)),
        out_shape=jax.ShapeDtypeStruct((n, MLA_DIM), out_dtype),
        scratch_shapes=[stat, stat, stat],
        compiler_params=_cparams(2),
        name="attn_prompt",
    )(q, k, wuv, g)


def _attn_sample_kernel(pt_ref, q_ref, knew_ref, ckv_hbm, kr_hbm, wuv_ref, g_ref, o_ref,
                        kbuf, rbuf, sems, *, layer, n_pages, t_new):
    b = pl.program_id(0)
    nb = pl.num_programs(0)
    past = n_pages * PAGE_SIZE

    def page_copies(seq, slot, page_of):
        for j in range(n_pages):
            pg = page_of(seq, j)
            rows = pl.ds(j * PAGE_SIZE, PAGE_SIZE)
            yield pltpu.make_async_copy(ckv_hbm.at[layer, pg], kbuf.at[slot, rows, :], sems.at[0, slot])
            yield pltpu.make_async_copy(kr_hbm.at[layer, pg], rbuf.at[slot, :, rows], sems.at[1, slot])

    def start_fetch(seq, slot):
        for cp in page_copies(seq, slot, lambda s, j: pt_ref[s, j]):
            cp.start()

    def wait_fetch(slot):
        for cp in page_copies(0, slot, lambda s, j: 0):
            cp.wait()

    @pl.when(b == 0)
    def _():
        start_fetch(0, 0)

    @pl.when(b + 1 < nb)
    def _():
        start_fetch(b + 1, (b + 1) % 2)

    slot = b % 2
    wait_fetch(slot)

    rows = MLA_HEADS * t_new
    q = q_ref[...].reshape(rows, QK_WIDTH).astype(BF)
    kf = kbuf[slot].astype(BF)
    rf = rbuf[slot].astype(BF)
    s = (_dot_nt(q[:, :MLA_KV_LORA], kf)
         + _dot(q[:, MLA_KV_LORA:MLA_KV_LORA + MLA_ROPE], rf)) * MLA_SCALE
    kn = jnp.concatenate([knew_ref[...], jnp.zeros((LANES - t_new, QK_WIDTH), knew_ref.dtype)],
                         axis=0).astype(BF)
    sn = _dot_nt(q, kn) * MLA_SCALE
    t_q = lax.broadcasted_iota(jnp.int32, (1, t_new, LANES), 1)
    t_k = lax.broadcasted_iota(jnp.int32, (1, t_new, LANES), 2)
    sn = jnp.where(t_k <= t_q, sn.reshape(MLA_HEADS, t_new, LANES), -jnp.inf).reshape(rows, LANES)
    m = jnp.maximum(jnp.max(s, axis=1, keepdims=True), jnp.max(sn, axis=1, keepdims=True))
    p = jnp.exp(s - m)
    pn = jnp.exp(sn - m)
    denom = jnp.sum(p, axis=1, keepdims=True) + jnp.sum(pn, axis=1, keepdims=True)
    o_lat = (_dot(p.astype(BF), kf) + _dot(pn.astype(BF), kn[:, :MLA_KV_LORA])) / denom
    o_ref[...] = _value_up_norm(o_lat, t_new, wuv_ref, g_ref).astype(o_ref.dtype)


def _attn_sample(page_table, q, knew, cache_kv, cache_kr, wuv, g, layer, out_dtype):
    nseq, n_pages = page_table.shape
    n = knew.shape[0]
    t_new = n // nseq
    past = n_pages * PAGE_SIZE
    grid_spec = pltpu.PrefetchScalarGridSpec(
        num_scalar_prefetch=1,
        grid=(nseq,),
        in_specs=[pl.BlockSpec((MLA_HEADS, t_new, QK_WIDTH), lambda b, pt: (0, b, 0)),
                  pl.BlockSpec((t_new, QK_WIDTH), lambda b, pt: (b, 0)),
                  pl.BlockSpec(memory_space=pl.ANY),
                  pl.BlockSpec(memory_space=pl.ANY),
                  pl.BlockSpec(wuv.shape, lambda b, pt: (0, 0, 0)),
                  pl.BlockSpec((1, MLA_DIM), lambda b, pt: (0, 0))],
        out_specs=pl.BlockSpec((t_new, MLA_DIM), lambda b, pt: (b, 0)),
        scratch_shapes=[pltpu.VMEM((2, past, MLA_KV_LORA), F32),
                        pltpu.VMEM((2, MLA_ROPE, past), F32),
                        pltpu.SemaphoreType.DMA((2, 2))],
    )
    return pl.pallas_call(
        functools.partial(_attn_sample_kernel, layer=layer, n_pages=n_pages, t_new=t_new),
        grid_spec=grid_spec,
        out_shape=jax.ShapeDtypeStruct((n, MLA_DIM), out_dtype),
        compiler_params=_cparams(1),
        name="attn_sample",
    )(page_table, q, knew, cache_kv, cache_kr, wuv, g)


def _out_ffn_kernel(x_ref, pool_ref, ssd_ref, mla_ref, wout_ref, g_ref, wup_ref, wdn_ref, o_ref,
                    xn_sc, hn_sc, acc_sc):
    k = pl.program_id(1)

    @pl.when(k == 0)
    def _():
        y = (_dot(pool_ref[...].astype(BF), wout_ref[0:POOL_DIM])
             + _dot(ssd_ref[...].astype(BF), wout_ref[POOL_DIM:POOL_DIM + SSD_DIM])
             + _dot(mla_ref[...].astype(BF), wout_ref[POOL_DIM + SSD_DIM:]))
        xn = x_ref[...] + y
        xn_sc[...] = xn
        hn_sc[...] = _rms(xn, g_ref[...]).astype(BF)
        acc_sc[...] = jnp.zeros_like(acc_sc)

    a = jnp.maximum(_dot(hn_sc[...], wup_ref[...]), 0.0)
    acc_sc[...] += _dot((a * a).astype(BF), wdn_ref[...])

    @pl.when(k == pl.num_programs(1) - 1)
    def _():
        o_ref[...] = xn_sc[...] + acc_sc[...]


def _out_ffn(x, pool, ssd, mla, wout, g, wup, wdn):
    n = x.shape[0]
    tm = _row_tile(n, 512)
    tf = 1024
    rowblk = lambda i, k: (i, 0)
    par = lambda i, k: (0, 0)
    return pl.pallas_call(
        _out_ffn_kernel,
        grid=(n // tm, FFN_DIM // tf),
        in_specs=[pl.BlockSpec((tm, D_MODEL), rowblk),
                  pl.BlockSpec((tm, POOL_DIM), rowblk),
                  pl.BlockSpec((tm, SSD_DIM), rowblk),
                  pl.BlockSpec((tm, MLA_DIM), rowblk),
                  pl.BlockSpec((D_MODEL, D_MODEL), par),
                  pl.BlockSpec((1, D_MODEL), par),
                  pl.BlockSpec((D_MODEL, tf), lambda i, k: (0, k)),
                  pl.BlockSpec((tf, D_MODEL), lambda i, k: (k, 0))],
        out_specs=pl.BlockSpec((tm, D_MODEL), rowblk),
        out_shape=jax.ShapeDtypeStruct((n, D_MODEL), F32),
        scratch_shapes=[pltpu.VMEM((tm, D_MODEL), F32), pltpu.VMEM((tm, D_MODEL), BF),
                        pltpu.VMEM((tm, D_MODEL), F32)],
        compiler_params=_cparams(2),
        name="out_ffn",
    )(x, pool, ssd, mla, wout, g, wup, wdn)


def _final_norm_kernel(x_ref, g_ref, o_ref):
    o_ref[...] = _rms(x_ref[...], g_ref[...])


def _final_norm(x, g):
    n = x.shape[0]
    tm = _row_tile(n, 512)
    return pl.pallas_call(
        _final_norm_kernel,
        grid=(n // tm,),
        in_specs=[pl.BlockSpec((tm, D_MODEL), lambda i: (i, 0)),
                  pl.BlockSpec((1, D_MODEL), lambda i: (0, 0))],
        out_specs=pl.BlockSpec((tm, D_MODEL), lambda i: (i, 0)),
        out_shape=jax.ShapeDtypeStruct((n, D_MODEL), F32),
        compiler_params=_cparams(1),
        name="final_norm",
    )(x, g)


def _pad_cols(a, width):
    return jnp.pad(a, [(0, 0)] * (a.ndim - 1) + [(0, width - a.shape[-1])])


def _rot_half_cols(w):
    half = MLA_ROPE // 2
    return jnp.concatenate([-w[..., half:], w[..., :half]], axis=-1)


def _prep_weights(w_in, pool_w, w_uq, w_uk, w_uv, w_out, w_up, w_down):
    depth = w_in.shape[0]
    parts, start = [], 0
    for wd in IN_WIDTHS:
        parts.append(w_in[..., start:start + wd])
        start += wd
    u, z, xbc, dt, cq, ckv, kpe = parts
    w_in_p = jnp.concatenate([u, z, xbc, cq, ckv, _pad_cols(kpe, LANES), _pad_cols(_rot_half_cols(kpe), LANES),
                              _pad_cols(dt, LANES)], axis=-1).astype(BF)
    eye4 = jnp.eye(len(POOL_WINDOWS), dtype=F32)
    pool_bd = (pool_w[:, :, :, None, :] * eye4[None, :, None, :, None]).reshape(depth, POOL_DIM, POOL_DIM).astype(BF)
    wq = w_uq.reshape(depth, MLA_Q_LORA, MLA_HEADS, MLA_NOPE + MLA_ROPE)
    wq_nope = wq[..., :MLA_NOPE].reshape(depth, MLA_Q_LORA, Q_NOPE_W)
    wq_pe = wq[..., MLA_NOPE:]
    wq_p = jnp.concatenate([wq_nope,
                            _pad_cols(wq_pe, LANES).reshape(depth, MLA_Q_LORA, Q_PE_W),
                            _pad_cols(_rot_half_cols(wq_pe), LANES).reshape(depth, MLA_Q_LORA, Q_PE_W)],
                           axis=-1).astype(BF)
    eye6 = jnp.eye(MLA_HEADS, dtype=F32)
    uk = jnp.transpose(w_uk, (0, 2, 3, 1))
    wuk_bd = (uk[:, :, :, None, :] * eye6[None, :, None, :, None]).reshape(
        depth, Q_NOPE_W, MLA_HEADS * MLA_KV_LORA).astype(BF)
    uv = jnp.transpose(w_uv, (0, 2, 1, 3))
    wuv_pad = (uv[:, :, :, None, :] * eye6[None, :, None, :, None]).reshape(
        depth, MLA_HEADS, MLA_KV_LORA, MLA_DIM).astype(BF)
    return w_in_p, pool_bd, wq_p, wuk_bd, wuv_pad, w_out.astype(BF), w_up.astype(BF), w_down.astype(BF)


def _rope_tables(pos, reps):
    half = MLA_ROPE // 2
    inv = ROPE_THETA ** (-jnp.arange(half, dtype=F32) * (2.0 / MLA_ROPE))
    ang = pos.astype(F32)[:, None] * inv[None, :]
    cos = jnp.cos(ang)
    sin = jnp.sin(ang)
    cos = _pad_cols(jnp.concatenate([cos, cos], axis=-1), LANES)
    sin = _pad_cols(jnp.concatenate([sin, sin], axis=-1), LANES)
    return jnp.tile(cos, (reps, 1)), jnp.tile(sin, (reps, 1))


def _state_to_rows(h):
    return _pad_cols(h.reshape(h.shape[0], SSD_DIM, SSD_STATE), LANES)


def _group_layer(x, batch, lw, pos0, pool_hist, conv_hist, ssm_h0, tables, attend, bb, act_dtype):
    n = x.shape[0]
    l = n // batch
    u, z, xbc, cq, ckv, kpe, kper, dt = _in_proj(x, lw["norm_mix_g"], lw["w_in"])

    pool_out, pool_tail = _pool(u.reshape(batch, l, POOL_DIM), pool_hist, lw["pool_bd"], lw["pool_scale"],
                                pos0, act_dtype)
    ssd_out, conv_tail, hbig = _ssd(z.reshape(batch, l, SSD_DIM), xbc.reshape(batch, l, SSD_CONV_DIM),
                                    dt.reshape(batch, l, LANES), conv_hist, ssm_h0,
                                    lw["conv_w"], lw["conv_b"], lw["dt_bias"], lw["a_log"], lw["d_skip"],
                                    lw["ssd_norm_g"], bb, act_dtype)
    q, kcat, ckv_n, kpe_r = _mla_prep(cq, ckv, kpe, kper, tables[0], tables[1], lw["q_norm_g"], lw["kv_norm_g"],
                                      lw["wq"], lw["wuk_bd"], act_dtype)
    mla_out = attend(q, kcat)
    x_new = _out_ffn(x, pool_out.reshape(n, POOL_DIM), ssd_out.reshape(n, SSD_DIM), mla_out,
                     lw["w_out"], lw["norm_ffn_g"], lw["w_up"], lw["w_down"])
    new_pool = pool_tail[:, HIST_ROWS - POOL_HIST:]
    new_conv = conv_tail[:, CONV_ROWS - (SSD_CONV - 1):]
    new_ssm = hbig.reshape(batch, SSD_HEADS, SSD_HEADDIM, SSD_STATE)
    return (x_new, ckv_n.reshape(batch, l, MLA_KV_LORA), kpe_r.reshape(batch, l, MLA_ROPE),
            new_pool, new_conv, new_ssm)


def kernel(x_prompt, x_sample, cache_kv_latent, cache_k_rope, state_pool, state_conv, state_ssm, page_table,
           norm_mix_g, w_in, pool_w, pool_scale, conv_w, conv_b, dt_bias, a_log, d_skip, ssd_norm_g, q_norm_g,
           w_uq, kv_norm_g, w_uk, w_uv, mla_out_g, w_out, norm_ffn_g, w_up, w_down, final_norm_g):
    depth = w_in.shape[0]
    bp, seq, _ = x_prompt.shape
    db, dec_seq, _ = x_sample.shape
    past_len = page_table.shape[1] * PAGE_SIZE

    w_in_p, pool_bd, wq_p, wuk_bd, wuv_pad, w_out_b, w_up_b, w_down_b = _prep_weights(
        w_in, pool_w, w_uq, w_uk, w_uv, w_out, w_up, w_down)
    tables_p = _rope_tables(jnp.arange(seq, dtype=jnp.int32), 1)
    tile_s = _row_tile(db * dec_seq, 512)
    tables_s = _rope_tables(past_len + jnp.arange(dec_seq, dtype=jnp.int32), tile_s // dec_seq)

    cache_kr_t = jnp.swapaxes(cache_k_rope, 2, 3)

    xp = x_prompt.reshape(bp * seq, D_MODEL)
    xs = x_sample.reshape(db * dec_seq, D_MODEL)
    zero_pool = jnp.zeros((bp, HIST_ROWS, POOL_DIM), F32)
    zero_conv = jnp.zeros((bp, CONV_ROWS, SSD_CONV_DIM), F32)
    zero_ssm = jnp.zeros((bp, SSD_DIM, LANES), F32)
    outs_p, outs_s = [], []
    for l in range(depth):
        lw = {
            "norm_mix_g": norm_mix_g[l][None], "w_in": w_in_p[l], "pool_bd": pool_bd[l],
            "pool_scale": pool_scale[l][None], "conv_w": conv_w[l], "conv_b": conv_b[l][None],
            "dt_bias": _pad_cols(dt_bias[l][None], LANES), "a_log": _pad_cols(a_log[l][None], LANES),
            "d_skip": jnp.repeat(d_skip[l], SSD_HEADDIM)[None], "ssd_norm_g": ssd_norm_g[l][None],
            "q_norm_g": q_norm_g[l][None], "kv_norm_g": kv_norm_g[l][None], "wq": wq_p[l], "wuk_bd": wuk_bd[l],
            "w_out": w_out_b[l], "norm_ffn_g": norm_ffn_g[l][None], "w_up": w_up_b[l], "w_down": w_down_b[l],
        }
        wuv_l = wuv_pad[l]
        og = mla_out_g[l][None]

        attend_p = lambda q, k: _attn_prompt(q, k, wuv_l, og, bp, BF)
        res = _group_layer(xp, bp, lw, 0, zero_pool, zero_conv, zero_ssm, tables_p, attend_p, 1, BF)
        xp = res[0]
        outs_p.append(res[1:])

        attend_s = lambda q, k: _attn_sample(page_table, q, k, cache_kv_latent, cache_kr_t, wuv_l, og, l, F32)
        pool_hist = jnp.pad(state_pool[l], ((0, 0), (HIST_ROWS - POOL_HIST, 0), (0, 0)))
        conv_hist = jnp.pad(state_conv[l], ((0, 0), (CONV_ROWS - (SSD_CONV - 1), 0), (0, 0)))
        res = _group_layer(xs, db, lw, past_len, pool_hist, conv_hist, _state_to_rows(state_ssm[l]),
                           tables_s, attend_s, 8, F32)
        xs = res[0]
        outs_s.append(res[1:])

    y_prompt = _final_norm(xp, final_norm_g[None]).reshape(bp, seq, D_MODEL)
    y_sample = _final_norm(xs, final_norm_g[None]).reshape(db, dec_seq, D_MODEL)
    stack = lambda outs, k: jnp.stack([o[k] for o in outs])
    return (y_prompt, y_sample,
            stack(outs_p, 0), stack(outs_p, 1), stack(outs_p, 2), stack(outs_p, 3), stack(outs_p, 4),
            stack(outs_s, 0), stack(outs_s, 1), stack(outs_s, 2), stack(outs_s, 3), stack(outs_s, 4))
```

```python
import functools

import jax
import jax.numpy as jnp
from jax import lax
from jax.experimental import pallas as pl
from jax.experimental.pallas import tpu as pltpu

D_MODEL = 1024
POOL_WINDOWS = (2, 4, 8, 16)
POOL_DIM = 256
POOL_GROUP = 64
POOL_HIST = 15
SSD_HEADDIM = 64
SSD_HEADS = 6
SSD_DIM = 384
SSD_GROUPS = 2
SSD_STATE = 64
SSD_CONV = 4
SSD_CONV_DIM = 640
SSD_CHUNK = 128
MLA_HEADS = 6
MLA_NOPE = 64
MLA_ROPE = 32
MLA_V = 64
MLA_Q_LORA = 256
MLA_KV_LORA = 128
MLA_DIM = 384
MLA_SCALE = (MLA_NOPE + MLA_ROPE) ** -0.5
ROPE_THETA = 10000.0
FFN_DIM = 4096
NORM_EPS = 1e-6
PAGE_SIZE = 128
IN_WIDTHS = (POOL_DIM, SSD_DIM, SSD_CONV_DIM, SSD_HEADS, MLA_Q_LORA, MLA_KV_LORA, MLA_ROPE)

LANES = 128
SUBLANES = 8
HIST_ROWS = 16
CONV_ROWS = 8
QK_WIDTH = 2 * LANES
VMEM_LIMIT = 48 * 1024 * 1024
ATTN_BQ = 256
ATTN_BK = 512
ATTN_HEAD_UNROLL = 6
LOG2E = 1.4426950408889634
POOL_TILE = 1024
FFN_ROWS = 1024
FFN_COLS = 1024
SSD_SEQ_UNROLL = 1

BF = jnp.bfloat16
F32 = jnp.float32


def _cparams(n_axes):
    return pltpu.CompilerParams(dimension_semantics=("arbitrary",) * n_axes,
                                vmem_limit_bytes=VMEM_LIMIT)


def _rms(x, g):
    return x * lax.rsqrt(jnp.mean(x * x, axis=-1, keepdims=True) + NORM_EPS) * g


def _dot(a, b):
    return jnp.dot(a, b, preferred_element_type=F32)


def _dot_nt(a, b):
    return lax.dot_general(a, b, (((1,), (1,)), ((), ())), preferred_element_type=F32)


def _row_tile(n, target):
    t = min(n, target)
    while n % t:
        t //= 2
    return t


IN_OUT_WIDTHS = (POOL_DIM, SSD_DIM, SSD_CONV_DIM, MLA_Q_LORA, MLA_KV_LORA, LANES, LANES, LANES)


def _in_proj_kernel(x_ref, g_ref, w_ref, *out_refs):
    h = _rms(x_ref[...], g_ref[...]).astype(BF)
    off = 0
    for o_ref in out_refs:
        width = o_ref.shape[-1]
        o_ref[...] = _dot(h, w_ref[:, off:off + width])
        off += width


def _in_proj(x, g, w):
    n = x.shape[0]
    tm = _row_tile(n, 512)
    wtot = w.shape[1]
    return pl.pallas_call(
        _in_proj_kernel,
        grid=(n // tm,),
        in_specs=[pl.BlockSpec((tm, D_MODEL), lambda i: (i, 0)),
                  pl.BlockSpec((1, D_MODEL), lambda i: (0, 0)),
                  pl.BlockSpec((D_MODEL, wtot), lambda i: (0, 0))],
        out_specs=[pl.BlockSpec((tm, wd), lambda i: (i, 0)) for wd in IN_OUT_WIDTHS],
        out_shape=[jax.ShapeDtypeStruct((n, wd), F32) for wd in IN_OUT_WIDTHS],
        compiler_params=_cparams(1),
        name="in_proj",
    )(x, g, w)


def _pool_kernel(u_ref, hist_ref, w_ref, scale_ref, out_ref, newhist_ref, carry_sc, *, pos0, tile, bb):
    t = pl.program_id(1)
    lane = lax.broadcasted_iota(jnp.int32, (1, POOL_DIM), 1)
    g0, g1, g2 = lane < POOL_GROUP, lane < 2 * POOL_GROUP, lane < 3 * POOL_GROUP
    width = jnp.where(g0, 2, jnp.where(g1, 4, jnp.where(g2, 8, 16)))
    pos = pos0 + t * tile + lax.broadcasted_iota(jnp.int32, (tile, 1), 0)
    cnt = jnp.minimum(width, pos + 1).astype(F32)

    def per_seq(i, carry):
        @pl.when(t == 0)
        def _():
            carry_sc[i] = hist_ref[i]

        u = u_ref[i]
        ext = jnp.concatenate([carry_sc[i], u], axis=0)
        s2 = ext + pltpu.roll(ext, 1, 0)
        s4 = s2 + pltpu.roll(s2, 2, 0)
        s8 = s4 + pltpu.roll(s4, 4, 0)
        s16 = s8 + pltpu.roll(s8, 8, 0)
        wsum = jnp.where(g0, s2, jnp.where(g1, s4, jnp.where(g2, s8, s16)))[HIST_ROWS:]
        m = wsum / cnt - u
        out = _dot(m.astype(BF), w_ref[...]) * scale_ref[...]
        out_ref[i] = out.astype(out_ref.dtype)
        last = ext[tile:tile + HIST_ROWS]
        carry_sc[i] = last
        newhist_ref[i] = last
        return carry

    if bb == 1:
        per_seq(0, 0)
    else:
        lax.fori_loop(0, bb, per_seq, 0)


def _pool(u, hist, wbd, scale, pos0, out_dtype):
    b, l, _ = u.shape
    tile = _row_tile(l, POOL_TILE)
    bb = _row_tile(b, max(1, POOL_TILE // l))
    return pl.pallas_call(
        functools.partial(_pool_kernel, pos0=pos0, tile=tile, bb=bb),
        grid=(b // bb, l // tile),
        in_specs=[pl.BlockSpec((bb, tile, POOL_DIM), lambda i, t: (i, t, 0)),
                  pl.BlockSpec((bb, HIST_ROWS, POOL_DIM), lambda i, t: (i, 0, 0)),
                  pl.BlockSpec((POOL_DIM, POOL_DIM), lambda i, t: (0, 0)),
                  pl.BlockSpec((1, POOL_DIM), lambda i, t: (0, 0))],
        out_specs=[pl.BlockSpec((bb, tile, POOL_DIM), lambda i, t: (i, t, 0)),
                   pl.BlockSpec((bb, HIST_ROWS, POOL_DIM), lambda i, t: (i, 0, 0))],
        out_shape=[jax.ShapeDtypeStruct((b, l, POOL_DIM), out_dtype),
                   jax.ShapeDtypeStruct((b, HIST_ROWS, POOL_DIM), F32)],
        scratch_shapes=[pltpu.VMEM((bb, HIST_ROWS, POOL_DIM), F32)],
        compiler_params=_cparams(2),
        name="pool_mix",
    )(u, hist, wbd, scale)


def _expand_heads(x):
    q = x.shape[0]
    col = lax.broadcasted_iota(jnp.int32, (1, SSD_DIM), 1)
    out = jnp.zeros((q, SSD_DIM), F32)
    for h in range(SSD_HEADS):
        sel = (col >= h * SSD_HEADDIM) & (col < (h + 1) * SSD_HEADDIM)
        out = jnp.where(sel, jnp.broadcast_to(x[:, h:h + 1], (q, SSD_DIM)), out)
    return out


def _ssd_kernel(z_ref, xbc_ref, dt_ref, hist_ref, h0_ref, convw_ref, convb_ref, dtb_ref, alog_ref,
                dskip_ref, ng_ref, y_ref, newconv_ref, ht_ref, ext_sc, hbig_sc, *, rows, bb):
    c = pl.program_id(1)
    q = SSD_CHUNK
    pad = q - rows

    def padded(x):
        if pad == 0:
            return x
        return jnp.concatenate([x, jnp.zeros((pad, x.shape[1]), x.dtype)], axis=0)

    row = lax.broadcasted_iota(jnp.int32, (q, 1), 0)
    col_t = lax.broadcasted_iota(jnp.int32, (1, q), 1)
    causal = row >= col_t
    lane = lax.broadcasted_iota(jnp.int32, (1, LANES), 1)
    col384 = lax.broadcasted_iota(jnp.int32, (1, SSD_DIM), 1)
    row384 = lax.broadcasted_iota(jnp.int32, (SSD_DIM, 1), 0)
    upper_rows = row384 >= SSD_DIM // SSD_GROUPS
    state_valid = jnp.where(upper_rows, 1, 0) == jnp.where(lane >= SSD_STATE, 1, 0)
    a_neg = -jnp.exp(alog_ref[...])

    def per_seq(i, carry):
        @pl.when(c == 0)
        def _():
            ext_sc[i, 0:CONV_ROWS, :] = hist_ref[i]
            h0 = h0_ref[i]
            hbig_sc[i] = jnp.where(upper_rows, pltpu.roll(h0, SSD_STATE, 1), h0)

        ext_sc[i, CONV_ROWS:CONV_ROWS + q, :] = padded(xbc_ref[i])
        conv = convb_ref[...]
        for k in range(SSD_CONV):
            start = CONV_ROWS - (SSD_CONV - 1) + k
            conv = conv + convw_ref[k:k + 1, :] * ext_sc[i, start:start + q, :]
        tail = ext_sc[i, rows:rows + CONV_ROWS, :]
        newconv_ref[i] = tail
        ext_sc[i, 0:CONV_ROWS, :] = tail

        act = conv * jax.nn.sigmoid(conv)
        xs = act[:, :SSD_DIM]
        bmat = act[:, SSD_DIM:SSD_DIM + LANES]
        cmat = act[:, SSD_DIM + LANES:]

        dtr = padded(dt_ref[i]) + dtb_ref[...]
        dt = jnp.maximum(dtr, 0.0) + jnp.log(1.0 + jnp.exp(-jnp.abs(dtr)))
        dt = jnp.where(row < rows, dt, 0.0)
        a_cs = dt * a_neg
        k = 1
        while k < q:
            a_cs = a_cs + jnp.where(row >= k, pltpu.roll(a_cs, k, 0), 0.0)
            k *= 2
        a_cs_t = a_cs.T
        a_last = a_cs[q - 1:q, :]
        decay_to_end = jnp.exp(a_last - a_cs)
        decay_from_start = jnp.exp(a_cs)
        chunk_decay = jnp.exp(a_last)

        dt_cols = _expand_heads(dt)
        xdt = xs * dt_cols
        xdec = xs * _expand_heads(dt * decay_to_end)

        bmat_b = bmat.astype(BF)
        cb = [_dot_nt(jnp.where((lane >= g * SSD_STATE) & (lane < (g + 1) * SSD_STATE), cmat, 0.0).astype(BF),
                      bmat_b) for g in range(SSD_GROUPS)]
        y = jnp.zeros((q, SSD_DIM), F32)
        for h in range(SSD_HEADS):
            seg = a_cs[:, h:h + 1] - a_cs_t[h:h + 1, :]
            lmat = jnp.exp(jnp.where(causal, seg, -jnp.inf))
            m = cb[h // (SSD_HEADS // SSD_GROUPS)] * lmat
            sel = (col384 >= h * SSD_HEADDIM) & (col384 < (h + 1) * SSD_HEADDIM)
            y = y + _dot(m.astype(BF), jnp.where(sel, xdt, 0.0).astype(BF))

        hbig = hbig_sc[i]
        y_off = _dot_nt(cmat.astype(BF), hbig.astype(BF)) * _expand_heads(decay_from_start)
        y = y + y_off + dskip_ref[...] * xs
        zf = padded(z_ref[i])
        gated = y * (zf * jax.nn.sigmoid(zf))
        y_ref[i] = _rms(gated, ng_ref[...])[:rows].astype(y_ref.dtype)

        new_states = _dot(xdec.T.astype(BF), bmat_b)
        cd_rows = _expand_heads(jnp.broadcast_to(chunk_decay, (q, LANES))).T[:, :LANES]
        hnew = hbig * cd_rows + jnp.where(state_valid, new_states, 0.0)
        hbig_sc[i] = hnew
        ht_ref[i] = jnp.where(upper_rows, pltpu.roll(hnew, SSD_STATE, 1), hnew)[:, :SSD_STATE]
        return carry

    if bb <= SSD_SEQ_UNROLL:
        for i in range(bb):
            per_seq(i, 0)
    else:
        lax.fori_loop(0, bb, per_seq, 0)


def _ssd(z, xbc, dt, hist, hbig0, convw, convb, dtb, alog, dskip, ng, bb, out_dtype):
    b, l, _ = z.shape
    rows = min(l, SSD_CHUNK)
    nc = l // rows
    seq = lambda i, c: (i, c, 0)
    fixed3 = lambda i, c: (i, 0, 0)
    par = lambda i, c: (0, 0)
    return pl.pallas_call(
        functools.partial(_ssd_kernel, rows=rows, bb=bb),
        grid=(b // bb, nc),
        in_specs=[pl.BlockSpec((bb, rows, SSD_DIM), seq),
                  pl.BlockSpec((bb, rows, SSD_CONV_DIM), seq),
                  pl.BlockSpec((bb, rows, LANES), seq),
                  pl.BlockSpec((bb, CONV_ROWS, SSD_CONV_DIM), fixed3),
                  pl.BlockSpec((bb, SSD_DIM, LANES), fixed3),
                  pl.BlockSpec((SSD_CONV, SSD_CONV_DIM), par),
                  pl.BlockSpec((1, SSD_CONV_DIM), par),
                  pl.BlockSpec((1, LANES), par),
                  pl.BlockSpec((1, LANES), par),
                  pl.BlockSpec((1, SSD_DIM), par),
                  pl.BlockSpec((1, SSD_DIM), par)],
        out_specs=[pl.BlockSpec((bb, rows, SSD_DIM), seq),
                   pl.BlockSpec((bb, CONV_ROWS, SSD_CONV_DIM), fixed3),
                   pl.BlockSpec((bb, SSD_DIM, SSD_STATE), fixed3)],
        out_shape=[jax.ShapeDtypeStruct((b, l, SSD_DIM), out_dtype),
                   jax.ShapeDtypeStruct((b, CONV_ROWS, SSD_CONV_DIM), F32),
                   jax.ShapeDtypeStruct((b, SSD_DIM, SSD_STATE), F32)],
        scratch_shapes=[pltpu.VMEM((bb, CONV_ROWS + SSD_CHUNK, SSD_CONV_DIM), F32),
                        pltpu.VMEM((bb, SSD_DIM, LANES), F32)],
        compiler_params=_cparams(2),
        name="ssd_mix",
    )(z, xbc, dt, hist, hbig0, convw, convb, dtb, alog, dskip, ng)


Q_NOPE_W = MLA_HEADS * MLA_NOPE
Q_PE_W = MLA_HEADS * LANES


def _mla_prep_kernel(cq_ref, ckv_ref, kpe_ref, kper_ref, cos_ref, sin_ref, qg_ref, kg_ref, wq_ref, wuk_ref,
                     q_ref, k_ref, ckvn_ref, kr_ref):
    hq = _rms(cq_ref[...], qg_ref[...]).astype(BF)
    q_nope = _dot(hq, wq_ref[:, :Q_NOPE_W])
    q_pe = _dot(hq, wq_ref[:, Q_NOPE_W:Q_NOPE_W + Q_PE_W])
    q_pe_rot = _dot(hq, wq_ref[:, Q_NOPE_W + Q_PE_W:])
    q_lat = _dot(q_nope.astype(BF), wuk_ref[...])
    cos = cos_ref[...]
    sin = sin_ref[...]
    for h in range(MLA_HEADS):
        sl = slice(h * LANES, (h + 1) * LANES)
        q_ref[h, :, 0:LANES] = q_lat[:, sl].astype(q_ref.dtype)
        q_ref[h, :, LANES:QK_WIDTH] = (q_pe[:, sl] * cos + q_pe_rot[:, sl] * sin).astype(q_ref.dtype)
    ckvn = _rms(ckv_ref[...], kg_ref[...])
    ckvn_ref[...] = ckvn
    kr = kpe_ref[...] * cos + kper_ref[...] * sin
    kr_ref[...] = kr[:, :MLA_ROPE]
    k_ref[:, 0:LANES] = ckvn.astype(k_ref.dtype)
    lane = lax.broadcasted_iota(jnp.int32, (1, LANES), 1)
    k_ref[:, LANES:QK_WIDTH] = jnp.where(lane == LANES - 1, 1.0, kr).astype(k_ref.dtype)


def _mla_prep(cq, ckv, kpe, kper, cos, sin, qg, kg, wq, wuk, qk_dtype):
    n = cq.shape[0]
    tm = _row_tile(n, 512)
    ntab = cos.shape[0] // tm
    rowblk = lambda i: (i, 0)
    tab = lambda i: (i % ntab, 0)
    par = lambda i: (0, 0)
    return pl.pallas_call(
        _mla_prep_kernel,
        grid=(n // tm,),
        in_specs=[pl.BlockSpec((tm, MLA_Q_LORA), rowblk),
                  pl.BlockSpec((tm, MLA_KV_LORA), rowblk),
                  pl.BlockSpec((tm, LANES), rowblk),
                  pl.BlockSpec((tm, LANES), rowblk),
                  pl.BlockSpec((tm, LANES), tab),
                  pl.BlockSpec((tm, LANES), tab),
                  pl.BlockSpec((1, MLA_Q_LORA), par),
                  pl.BlockSpec((1, MLA_KV_LORA), par),
                  pl.BlockSpec(wq.shape, par),
                  pl.BlockSpec(wuk.shape, par)],
        out_specs=[pl.BlockSpec((MLA_HEADS, tm, QK_WIDTH), lambda i: (0, i, 0)),
                   pl.BlockSpec((tm, QK_WIDTH), rowblk),
                   pl.BlockSpec((tm, MLA_KV_LORA), rowblk),
                   pl.BlockSpec((tm, MLA_ROPE), rowblk)],
        out_shape=[jax.ShapeDtypeStruct((MLA_HEADS, n, QK_WIDTH), qk_dtype),
                   jax.ShapeDtypeStruct((n, QK_WIDTH), qk_dtype),
                   jax.ShapeDtypeStruct((n, MLA_KV_LORA), F32),
                   jax.ShapeDtypeStruct((n, MLA_ROPE), F32)],
        compiler_params=_cparams(1),
        name="mla_prep",
    )(cq, ckv, kpe, kper, cos, sin, qg, kg, wq, wuk)


def _value_up_norm(o_lat, rows, wuv_ref, g_ref):
    out = jnp.zeros((rows, MLA_DIM), F32)
    for h in range(MLA_HEADS):
        out = out + _dot(o_lat[h * rows:(h + 1) * rows].astype(BF), wuv_ref[h])
    return _rms(out, g_ref[...])


def _attn_prompt_kernel(q_ref, k_ref, wuv_ref, g_ref, o_ref, m_sc, acc_sc, *, bq, bk, hu):
    i = pl.program_id(1)
    m_sc[...] = jnp.full(m_sc.shape, -jnp.inf, F32)
    acc_sc[...] = jnp.zeros(acc_sc.shape, F32)
    exp2_scale = MLA_SCALE * LOG2E

    def head_step(h, kb, mask):
        s = _dot_nt(q_ref[h], kb)
        if mask is not None:
            s = jnp.where(mask, s, -jnp.inf)
        m_old = m_sc[h]
        m_new = jnp.maximum(m_old, jnp.max(s, axis=1, keepdims=True))
        alpha = jnp.exp2((m_old - m_new) * exp2_scale)
        p = jnp.exp2((s - jnp.concatenate([m_new] * (s.shape[1] // LANES), axis=1)) * exp2_scale)
        acc_sc[h] = jnp.concatenate([alpha, alpha], axis=1) * acc_sc[h] + _dot(p.astype(BF), kb)
        m_sc[h] = m_new

    def kv_step(j, width, masked):
        start = pl.multiple_of(j * bk, bk)
        mask = None
        if masked:
            t_q = i * bq + lax.broadcasted_iota(jnp.int32, (bq, width), 0)
            t_k = start + lax.broadcasted_iota(jnp.int32, (bq, width), 1)
            mask = t_k <= t_q

        def head_group(gi, carry):
            kb = k_ref[pl.ds(start, width), :]
            for u in range(hu):
                head_step(gi * hu + u, kb, mask)
            return carry

        lax.fori_loop(0, MLA_HEADS // hu, head_group, 0)

    n_full = (i * bq) // bk

    def body(j, carry):
        kv_step(j, bk, False)
        return carry

    lax.fori_loop(0, n_full, body, 0)
    rest = (i + 1) * bq - n_full * bk
    for width in range(bq, bk + 1, bq):
        @pl.when(rest == width)
        def _():
            kv_step(n_full, width, True)
    out = jnp.zeros((bq, MLA_DIM), F32)
    for h in range(MLA_HEADS):
        acc = acc_sc[h]
        o_lat = acc[:, :MLA_KV_LORA] / acc[:, QK_WIDTH - 1:QK_WIDTH]
        out = out + _dot(o_lat.astype(BF), wuv_ref[h])
    o_ref[...] = _rms(out, g_ref[...]).astype(o_ref.dtype)


def _attn_prompt(q, k, wuv, g, batch, out_dtype):
    n = k.shape[0]
    l = n // batch
    bq = _row_tile(l, ATTN_BQ)
    bk = _row_tile(l, ATTN_BK)
    nq = l // bq
    return pl.pallas_call(
        functools.partial(_attn_prompt_kernel, bq=bq, bk=bk, hu=ATTN_HEAD_UNROLL),
        grid=(batch, nq),
        in_specs=[pl.BlockSpec((MLA_HEADS, bq, QK_WIDTH), lambda b, i: (0, b * nq + i, 0)),
                  pl.BlockSpec((l, QK_WIDTH), lambda b, i: (b, 0)),
                  pl.BlockSpec(wuv.shape, lambda b, i: (0, 0, 0)),
                  pl.BlockSpec((1, MLA_DIM), lambda b, i: (0, 0))],
        out_specs=pl.BlockSpec((bq, MLA_DIM), lambda b, i: (b * nq + i, 0)),
        out_shape=jax.ShapeDtypeStruct((n, MLA_DIM), out_dtype),
        scratch_shapes=[pltpu.VMEM((MLA_HEADS, bq, LANES), F32), pltpu.VMEM((MLA_HEADS, bq, QK_WIDTH), F32)],
        compiler_params=_cparams(2),
        name="attn_prompt",
    )(q, k, wuv, g)


def _attn_sample_kernel(pt_ref, q_ref, knew_ref, ckv_hbm, kr_hbm, wuv_ref, g_ref, o_ref,
                        kbuf, rbuf, sems, *, layer, n_pages, t_new):
    b = pl.program_id(0)
    nb = pl.num_programs(0)
    past = n_pages * PAGE_SIZE

    def page_copies(seq, slot, page_of):
        for j in range(n_pages):
            pg = page_of(seq, j)
            rows = pl.ds(j * PAGE_SIZE, PAGE_SIZE)
            yield pltpu.make_async_copy(ckv_hbm.at[layer, pg], kbuf.at[slot, rows, :], sems.at[0, slot])
            yield pltpu.make_async_copy(kr_hbm.at[layer, pg], rbuf.at[slot, :, rows], sems.at[1, slot])

    def start_fetch(seq, slot):
        for cp in page_copies(seq, slot, lambda s, j: pt_ref[s, j]):
            cp.start()

    def wait_fetch(slot):
        for cp in page_copies(0, slot, lambda s, j: 0):
            cp.wait()

    @pl.when(b == 0)
    def _():
        start_fetch(0, 0)

    @pl.when(b + 1 < nb)
    def _():
        start_fetch(b + 1, (b + 1) % 2)

    slot = b % 2
    wait_fetch(slot)

    rows = MLA_HEADS * t_new
    exp2_scale = MLA_SCALE * LOG2E
    q = q_ref[...].reshape(rows, QK_WIDTH).astype(BF)
    q_lat = q[:, :MLA_KV_LORA]
    q_pe = q[:, MLA_KV_LORA:MLA_KV_LORA + MLA_ROPE]

    kn = jnp.concatenate([knew_ref[...], jnp.zeros((LANES - t_new, QK_WIDTH), knew_ref.dtype)],
                         axis=0).astype(BF)
    sn = _dot_nt(q, kn)
    t_q = lax.broadcasted_iota(jnp.int32, (1, t_new, LANES), 1)
    t_k = lax.broadcasted_iota(jnp.int32, (1, t_new, LANES), 2)
    sn = jnp.where(t_k <= t_q, sn.reshape(MLA_HEADS, t_new, LANES), -jnp.inf).reshape(rows, LANES)
    kf = kbuf[slot].astype(BF)
    rf = rbuf[slot].astype(BF)
    s = _dot_nt(q_lat, kf) + _dot(q_pe, rf)
    m = jnp.maximum(jnp.max(s, axis=1, keepdims=True), jnp.max(sn, axis=1, keepdims=True))
    p = jnp.exp2((s - m) * exp2_scale)
    pn = jnp.exp2((sn - m) * exp2_scale)
    denom = jnp.sum(p, axis=1, keepdims=True) + jnp.sum(pn, axis=1, keepdims=True)
    o_lat = (_dot(p.astype(BF), kf) + _dot(pn.astype(BF), kn[:, :MLA_KV_LORA])) / denom
    o_ref[...] = _value_up_norm(o_lat, t_new, wuv_ref, g_ref).astype(o_ref.dtype)


def _attn_sample(page_table, q, knew, cache_kv, cache_kr, wuv, g, layer, out_dtype):
    nseq, n_pages = page_table.shape
    n = knew.shape[0]
    t_new = n // nseq
    past = n_pages * PAGE_SIZE
    grid_spec = pltpu.PrefetchScalarGridSpec(
        num_scalar_prefetch=1,
        grid=(nseq,),
        in_specs=[pl.BlockSpec((MLA_HEADS, t_new, QK_WIDTH), lambda b, pt: (0, b, 0)),
                  pl.BlockSpec((t_new, QK_WIDTH), lambda b, pt: (b, 0)),
                  pl.BlockSpec(memory_space=pl.ANY),
                  pl.BlockSpec(memory_space=pl.ANY),
                  pl.BlockSpec(wuv.shape, lambda b, pt: (0, 0, 0)),
                  pl.BlockSpec((1, MLA_DIM), lambda b, pt: (0, 0))],
        out_specs=pl.BlockSpec((t_new, MLA_DIM), lambda b, pt: (b, 0)),
        scratch_shapes=[pltpu.VMEM((2, past, MLA_KV_LORA), F32),
                        pltpu.VMEM((2, MLA_ROPE, past), F32),
                        pltpu.SemaphoreType.DMA((2, 2))],
    )
    return pl.pallas_call(
        functools.partial(_attn_sample_kernel, layer=layer, n_pages=n_pages, t_new=t_new),
        grid_spec=grid_spec,
        out_shape=jax.ShapeDtypeStruct((n, MLA_DIM), out_dtype),
        compiler_params=_cparams(1),
        name="attn_sample",
    )(page_table, q, knew, cache_kv, cache_kr, wuv, g)


def _out_ffn_kernel(x_ref, pool_ref, ssd_ref, mla_ref, wout_ref, g_ref, wup_ref, wdn_ref, gf_ref, o_ref,
                    xn_sc, hn_sc, acc_sc, *, final):
    k = pl.program_id(1)

    @pl.when(k == 0)
    def _():
        y = (_dot(pool_ref[...].astype(BF), wout_ref[0:POOL_DIM])
             + _dot(ssd_ref[...].astype(BF), wout_ref[POOL_DIM:POOL_DIM + SSD_DIM])
             + _dot(mla_ref[...].astype(BF), wout_ref[POOL_DIM + SSD_DIM:]))
        xn = x_ref[...] + y
        xn_sc[...] = xn
        hn_sc[...] = _rms(xn, g_ref[...]).astype(BF)
        acc_sc[...] = jnp.zeros_like(acc_sc)

    a = jnp.maximum(_dot(hn_sc[...], wup_ref[...]), 0.0)
    acc_sc[...] += _dot((a * a).astype(BF), wdn_ref[...])

    @pl.when(k == pl.num_programs(1) - 1)
    def _():
        x_out = xn_sc[...] + acc_sc[...]
        o_ref[...] = _rms(x_out, gf_ref[...]) if final else x_out


def _out_ffn(x, pool, ssd, mla, wout, g, wup, wdn, gfinal, final):
    n = x.shape[0]
    tm = _row_tile(n, FFN_ROWS)
    tf = FFN_COLS
    rowblk = lambda i, k: (i, 0)
    par = lambda i, k: (0, 0)
    return pl.pallas_call(
        functools.partial(_out_ffn_kernel, final=final),
        grid=(n // tm, FFN_DIM // tf),
        in_specs=[pl.BlockSpec((tm, D_MODEL), rowblk),
                  pl.BlockSpec((tm, POOL_DIM), rowblk),
                  pl.BlockSpec((tm, SSD_DIM), rowblk),
                  pl.BlockSpec((tm, MLA_DIM), rowblk),
                  pl.BlockSpec((D_MODEL, D_MODEL), par),
                  pl.BlockSpec((1, D_MODEL), par),
                  pl.BlockSpec((D_MODEL, tf), lambda i, k: (0, k)),
                  pl.BlockSpec((tf, D_MODEL), lambda i, k: (k, 0)),
                  pl.BlockSpec((1, D_MODEL), par)],
        out_specs=pl.BlockSpec((tm, D_MODEL), rowblk),
        out_shape=jax.ShapeDtypeStruct((n, D_MODEL), F32),
        scratch_shapes=[pltpu.VMEM((tm, D_MODEL), F32), pltpu.VMEM((tm, D_MODEL), BF),
                        pltpu.VMEM((tm, D_MODEL), F32)],
        compiler_params=_cparams(2),
        name="out_ffn",
    )(x, pool, ssd, mla, wout, g, wup, wdn, gfinal)


def _pad_cols(a, width):
    return jnp.pad(a, [(0, 0)] * (a.ndim - 1) + [(0, width - a.shape[-1])])


def _rot_half_cols(w):
    half = MLA_ROPE // 2
    return jnp.concatenate([-w[..., half:], w[..., :half]], axis=-1)


def _prep_weights(w_in, pool_w, w_uq, w_uk, w_uv, w_out, w_up, w_down):
    depth = w_in.shape[0]
    parts, start = [], 0
    for wd in IN_WIDTHS:
        parts.append(w_in[..., start:start + wd])
        start += wd
    u, z, xbc, dt, cq, ckv, kpe = parts
    w_in_p = jnp.concatenate([u, z, xbc, cq, ckv, _pad_cols(kpe, LANES), _pad_cols(_rot_half_cols(kpe), LANES),
                              _pad_cols(dt, LANES)], axis=-1).astype(BF)
    eye4 = jnp.eye(len(POOL_WINDOWS), dtype=F32)
    pool_bd = (pool_w[:, :, :, None, :] * eye4[None, :, None, :, None]).reshape(depth, POOL_DIM, POOL_DIM).astype(BF)
    wq = w_uq.reshape(depth, MLA_Q_LORA, MLA_HEADS, MLA_NOPE + MLA_ROPE)
    wq_nope = wq[..., :MLA_NOPE].reshape(depth, MLA_Q_LORA, Q_NOPE_W)
    wq_pe = wq[..., MLA_NOPE:]
    wq_p = jnp.concatenate([wq_nope,
                            _pad_cols(wq_pe, LANES).reshape(depth, MLA_Q_LORA, Q_PE_W),
                            _pad_cols(_rot_half_cols(wq_pe), LANES).reshape(depth, MLA_Q_LORA, Q_PE_W)],
                           axis=-1).astype(BF)
    eye6 = jnp.eye(MLA_HEADS, dtype=F32)
    uk = jnp.transpose(w_uk, (0, 2, 3, 1))
    wuk_bd = (uk[:, :, :, None, :] * eye6[None, :, None, :, None]).reshape(
        depth, Q_NOPE_W, MLA_HEADS * MLA_KV_LORA).astype(BF)
    uv = jnp.transpose(w_uv, (0, 2, 1, 3))
    wuv_pad = (uv[:, :, :, None, :] * eye6[None, :, None, :, None]).reshape(
        depth, MLA_HEADS, MLA_KV_LORA, MLA_DIM).astype(BF)
    return w_in_p, pool_bd, wq_p, wuk_bd, wuv_pad, w_out.astype(BF), w_up.astype(BF), w_down.astype(BF)


def _rope_tables(pos, reps):
    half = MLA_ROPE // 2
    inv = ROPE_THETA ** (-jnp.arange(half, dtype=F32) * (2.0 / MLA_ROPE))
    ang = pos.astype(F32)[:, None] * inv[None, :]
    cos = jnp.cos(ang)
    sin = jnp.sin(ang)
    cos = _pad_cols(jnp.concatenate([cos, cos], axis=-1), LANES)
    sin = _pad_cols(jnp.concatenate([sin, sin], axis=-1), LANES)
    return jnp.tile(cos, (reps, 1)), jnp.tile(sin, (reps, 1))


def _state_to_rows(h):
    return _pad_cols(h.reshape(h.shape[0], SSD_DIM, SSD_STATE), LANES)


def _group_layer(x, batch, lw, pos0, pool_hist, conv_hist, ssm_h0, tables, attend, bb, act_dtype):
    n = x.shape[0]
    l = n // batch
    u, z, xbc, cq, ckv, kpe, kper, dt = _in_proj(x, lw["norm_mix_g"], lw["w_in"])

    pool_out, pool_tail = _pool(u.reshape(batch, l, POOL_DIM), pool_hist, lw["pool_bd"], lw["pool_scale"],
                                pos0, act_dtype)
    ssd_out, conv_tail, hbig = _ssd(z.reshape(batch, l, SSD_DIM), xbc.reshape(batch, l, SSD_CONV_DIM),
                                    dt.reshape(batch, l, LANES), conv_hist, ssm_h0,
                                    lw["conv_w"], lw["conv_b"], lw["dt_bias"], lw["a_log"], lw["d_skip"],
                                    lw["ssd_norm_g"], bb, act_dtype)
    q, kcat, ckv_n, kpe_r = _mla_prep(cq, ckv, kpe, kper, tables[0], tables[1], lw["q_norm_g"], lw["kv_norm_g"],
                                      lw["wq"], lw["wuk_bd"], act_dtype)
    mla_out = attend(q, kcat)
    x_new = _out_ffn(x, pool_out.reshape(n, POOL_DIM), ssd_out.reshape(n, SSD_DIM), mla_out,
                     lw["w_out"], lw["norm_ffn_g"], lw["w_up"], lw["w_down"], lw["final_g"], lw["final"])
    new_pool = pool_tail[:, HIST_ROWS - POOL_HIST:]
    new_conv = conv_tail[:, CONV_ROWS - (SSD_CONV - 1):]
    new_ssm = hbig.reshape(batch, SSD_HEADS, SSD_HEADDIM, SSD_STATE)
    return (x_new, ckv_n.reshape(batch, l, MLA_KV_LORA), kpe_r.reshape(batch, l, MLA_ROPE),
            new_pool, new_conv, new_ssm)


def kernel(x_prompt, x_sample, cache_kv_latent, cache_k_rope, state_pool, state_conv, state_ssm, page_table,
           norm_mix_g, w_in, pool_w, pool_scale, conv_w, conv_b, dt_bias, a_log, d_skip, ssd_norm_g, q_norm_g,
           w_uq, kv_norm_g, w_uk, w_uv, mla_out_g, w_out, norm_ffn_g, w_up, w_down, final_norm_g):
    depth = w_in.shape[0]
    bp, seq, _ = x_prompt.shape
    db, dec_seq, _ = x_sample.shape
    past_len = page_table.shape[1] * PAGE_SIZE

    w_in_p, pool_bd, wq_p, wuk_bd, wuv_pad, w_out_b, w_up_b, w_down_b = _prep_weights(
        w_in, pool_w, w_uq, w_uk, w_uv, w_out, w_up, w_down)
    tables_p = _rope_tables(jnp.arange(seq, dtype=jnp.int32), 1)
    tile_s = _row_tile(db * dec_seq, 512)
    tables_s = _rope_tables(past_len + jnp.arange(dec_seq, dtype=jnp.int32), tile_s // dec_seq)

    cache_kr_t = jnp.swapaxes(cache_k_rope, 2, 3)

    xp = x_prompt.reshape(bp * seq, D_MODEL)
    xs = x_sample.reshape(db * dec_seq, D_MODEL)
    zero_pool = jnp.zeros((bp, HIST_ROWS, POOL_DIM), F32)
    zero_conv = jnp.zeros((bp, CONV_ROWS, SSD_CONV_DIM), F32)
    zero_ssm = jnp.zeros((bp, SSD_DIM, LANES), F32)
    outs_p, outs_s = [], []
    for l in range(depth):
        lw = {
            "norm_mix_g": norm_mix_g[l][None], "w_in": w_in_p[l], "pool_bd": pool_bd[l],
            "pool_scale": pool_scale[l][None], "conv_w": conv_w[l], "conv_b": conv_b[l][None],
            "dt_bias": _pad_cols(dt_bias[l][None], LANES), "a_log": _pad_cols(a_log[l][None], LANES),
            "d_skip": jnp.repeat(d_skip[l], SSD_HEADDIM)[None], "ssd_norm_g": ssd_norm_g[l][None],
            "q_norm_g": q_norm_g[l][None], "kv_norm_g": kv_norm_g[l][None], "wq": wq_p[l], "wuk_bd": wuk_bd[l],
            "w_out": w_out_b[l], "norm_ffn_g": norm_ffn_g[l][None], "w_up": w_up_b[l], "w_down": w_down_b[l],
            "final_g": final_norm_g[None], "final": l == depth - 1,
        }
        wuv_l = wuv_pad[l]
        og = mla_out_g[l][None]

        attend_p = lambda q, k: _attn_prompt(q, k, wuv_l, og, bp, BF)
        res = _group_layer(xp, bp, lw, 0, zero_pool, zero_conv, zero_ssm, tables_p, attend_p, min(bp, SSD_SEQ_UNROLL), BF)
        xp = res[0]
        outs_p.append(res[1:])

        attend_s = lambda q, k: _attn_sample(page_table, q, k, cache_kv_latent, cache_kr_t, wuv_l, og, l, F32)
        pool_hist = jnp.pad(state_pool[l], ((0, 0), (HIST_ROWS - POOL_HIST, 0), (0, 0)))
        conv_hist = jnp.pad(state_conv[l], ((0, 0), (CONV_ROWS - (SSD_CONV - 1), 0), (0, 0)))
        res = _group_layer(xs, db, lw, past_len, pool_hist, conv_hist, _state_to_rows(state_ssm[l]),
                           tables_s, attend_s, 8, F32)
        xs = res[0]
        outs_s.append(res[1:])

    y_prompt = xp.reshape(bp, seq, D_MODEL)
    y_sample = xs.reshape(db, dec_seq, D_MODEL)
    stack = lambda outs, k: jnp.stack([o[k] for o in outs])
    return (y_prompt, y_sample,
            stack(outs_p, 0), stack(outs_p, 1), stack(outs_p, 2), stack(outs_p, 3), stack(outs_p, 4),
            stack(outs_s, 0), stack(outs_s, 1), stack(outs_s, 2), stack(outs_s, 3), stack(outs_s, 4))
```

```python
import functools

import jax
import jax.numpy as jnp
from jax import lax
from jax.experimental import pallas as pl
from jax.experimental.pallas import tpu as pltpu

D_MODEL = 1024
POOL_WINDOWS = (2, 4, 8, 16)
POOL_DIM = 256
POOL_GROUP = 64
POOL_HIST = 15
SSD_HEADDIM = 64
SSD_HEADS = 6
SSD_DIM = 384
SSD_GROUPS = 2
SSD_STATE = 64
SSD_CONV = 4
SSD_CONV_DIM = 640
SSD_CHUNK = 128
MLA_HEADS = 6
MLA_NOPE = 64
MLA_ROPE = 32
MLA_V = 64
MLA_Q_LORA = 256
MLA_KV_LORA = 128
MLA_DIM = 384
MLA_SCALE = (MLA_NOPE + MLA_ROPE) ** -0.5
ROPE_THETA = 10000.0
FFN_DIM = 4096
NORM_EPS = 1e-6
PAGE_SIZE = 128
IN_WIDTHS = (POOL_DIM, SSD_DIM, SSD_CONV_DIM, SSD_HEADS, MLA_Q_LORA, MLA_KV_LORA, MLA_ROPE)

LANES = 128
SUBLANES = 8
HIST_ROWS = 16
CONV_ROWS = 8
QK_WIDTH = 2 * LANES
VMEM_LIMIT = 48 * 1024 * 1024
ATTN_BQ = 512
ATTN_BK = 512
ATTN_HEAD_UNROLL = 6
LOG2E = 1.4426950408889634
POOL_TILE = 1024
FFN_ROWS = 1024
FFN_COLS = 1024
SSD_SEQ_UNROLL = 1

BF = jnp.bfloat16
F32 = jnp.float32


def _cparams(n_axes):
    return pltpu.CompilerParams(dimension_semantics=("arbitrary",) * n_axes,
                                vmem_limit_bytes=VMEM_LIMIT)


def _rms(x, g):
    return x * lax.rsqrt(jnp.mean(x * x, axis=-1, keepdims=True) + NORM_EPS) * g


def _dot(a, b):
    return jnp.dot(a, b, preferred_element_type=F32)


def _dot_nt(a, b):
    return lax.dot_general(a, b, (((1,), (1,)), ((), ())), preferred_element_type=F32)


def _row_tile(n, target):
    t = min(n, target)
    while n % t:
        t //= 2
    return t


IN_OUT_WIDTHS = (POOL_DIM, SSD_DIM, SSD_CONV_DIM, MLA_Q_LORA, MLA_KV_LORA, LANES, LANES, LANES)


def _in_proj_kernel(x_ref, g_ref, w_ref, *out_refs):
    h = _rms(x_ref[...], g_ref[...]).astype(BF)
    proj = _dot(h, w_ref[...])
    off = 0
    for o_ref in out_refs:
        width = o_ref.shape[-1]
        o_ref[...] = proj[:, off:off + width]
        off += width


def _in_proj(x, g, w):
    n = x.shape[0]
    tm = _row_tile(n, 512)
    wtot = w.shape[1]
    return pl.pallas_call(
        _in_proj_kernel,
        grid=(n // tm,),
        in_specs=[pl.BlockSpec((tm, D_MODEL), lambda i: (i, 0)),
                  pl.BlockSpec((1, D_MODEL), lambda i: (0, 0)),
                  pl.BlockSpec((D_MODEL, wtot), lambda i: (0, 0))],
        out_specs=[pl.BlockSpec((tm, wd), lambda i: (i, 0)) for wd in IN_OUT_WIDTHS],
        out_shape=[jax.ShapeDtypeStruct((n, wd), F32) for wd in IN_OUT_WIDTHS],
        compiler_params=_cparams(1),
        name="in_proj",
    )(x, g, w)


def _pool_kernel(u_ref, hist_ref, w_ref, scale_ref, out_ref, newhist_ref, carry_sc, *, pos0, tile, bb):
    t = pl.program_id(1)
    lane = lax.broadcasted_iota(jnp.int32, (1, POOL_DIM), 1)
    g0, g1, g2 = lane < POOL_GROUP, lane < 2 * POOL_GROUP, lane < 3 * POOL_GROUP
    width = jnp.where(g0, 2, jnp.where(g1, 4, jnp.where(g2, 8, 16)))
    pos = pos0 + t * tile + lax.broadcasted_iota(jnp.int32, (tile, 1), 0)
    cnt = jnp.minimum(width, pos + 1).astype(F32)

    def per_seq(i, carry):
        @pl.when(t == 0)
        def _():
            carry_sc[i] = hist_ref[i]

        u = u_ref[i]
        ext = jnp.concatenate([carry_sc[i], u], axis=0)
        s2 = ext + pltpu.roll(ext, 1, 0)
        s4 = s2 + pltpu.roll(s2, 2, 0)
        s8 = s4 + pltpu.roll(s4, 4, 0)
        s16 = s8 + pltpu.roll(s8, 8, 0)
        wsum = jnp.where(g0, s2, jnp.where(g1, s4, jnp.where(g2, s8, s16)))[HIST_ROWS:]
        m = wsum / cnt - u
        out = _dot(m.astype(BF), w_ref[...]) * scale_ref[...]
        out_ref[i] = out.astype(out_ref.dtype)
        last = ext[tile:tile + HIST_ROWS]
        carry_sc[i] = last
        newhist_ref[i] = last
        return carry

    if bb == 1:
        per_seq(0, 0)
    else:
        lax.fori_loop(0, bb, per_seq, 0)


def _pool(u, hist, wbd, scale, pos0, out_dtype):
    b, l, _ = u.shape
    tile = _row_tile(l, POOL_TILE)
    bb = _row_tile(b, max(1, POOL_TILE // l))
    return pl.pallas_call(
        functools.partial(_pool_kernel, pos0=pos0, tile=tile, bb=bb),
        grid=(b // bb, l // tile),
        in_specs=[pl.BlockSpec((bb, tile, POOL_DIM), lambda i, t: (i, t, 0)),
                  pl.BlockSpec((bb, HIST_ROWS, POOL_DIM), lambda i, t: (i, 0, 0)),
                  pl.BlockSpec((POOL_DIM, POOL_DIM), lambda i, t: (0, 0)),
                  pl.BlockSpec((1, POOL_DIM), lambda i, t: (0, 0))],
        out_specs=[pl.BlockSpec((bb, tile, POOL_DIM), lambda i, t: (i, t, 0)),
                   pl.BlockSpec((bb, HIST_ROWS, POOL_DIM), lambda i, t: (i, 0, 0))],
        out_shape=[jax.ShapeDtypeStruct((b, l, POOL_DIM), out_dtype),
                   jax.ShapeDtypeStruct((b, HIST_ROWS, POOL_DIM), F32)],
        scratch_shapes=[pltpu.VMEM((bb, HIST_ROWS, POOL_DIM), F32)],
        compiler_params=_cparams(2),
        name="pool_mix",
    )(u, hist, wbd, scale)


def _expand_heads(x):
    q = x.shape[0]
    col = lax.broadcasted_iota(jnp.int32, (1, SSD_DIM), 1)
    out = jnp.zeros((q, SSD_DIM), F32)
    for h in range(SSD_HEADS):
        sel = (col >= h * SSD_HEADDIM) & (col < (h + 1) * SSD_HEADDIM)
        out = jnp.where(sel, jnp.broadcast_to(x[:, h:h + 1], (q, SSD_DIM)), out)
    return out


def _ssd_kernel(z_ref, xbc_ref, dt_ref, hist_ref, h0_ref, convw_ref, convb_ref, dtb_ref, alog_ref,
                dskip_ref, ng_ref, y_ref, newconv_ref, ht_ref, ext_sc, hbig_sc, *, rows, bb):
    c = pl.program_id(1)
    q = SSD_CHUNK
    pad = q - rows

    def padded(x):
        if pad == 0:
            return x
        return jnp.concatenate([x, jnp.zeros((pad, x.shape[1]), x.dtype)], axis=0)

    row = lax.broadcasted_iota(jnp.int32, (q, 1), 0)
    col_t = lax.broadcasted_iota(jnp.int32, (1, q), 1)
    causal = row >= col_t
    lane = lax.broadcasted_iota(jnp.int32, (1, LANES), 1)
    col384 = lax.broadcasted_iota(jnp.int32, (1, SSD_DIM), 1)
    row384 = lax.broadcasted_iota(jnp.int32, (SSD_DIM, 1), 0)
    upper_rows = row384 >= SSD_DIM // SSD_GROUPS
    state_valid = jnp.where(upper_rows, 1, 0) == jnp.where(lane >= SSD_STATE, 1, 0)
    a_neg = -jnp.exp(alog_ref[...])

    def per_seq(i, carry):
        @pl.when(c == 0)
        def _():
            ext_sc[i, 0:CONV_ROWS, :] = hist_ref[i]
            h0 = h0_ref[i]
            hbig_sc[i] = jnp.where(upper_rows, pltpu.roll(h0, SSD_STATE, 1), h0)

        ext_sc[i, CONV_ROWS:CONV_ROWS + q, :] = padded(xbc_ref[i])
        ext = ext_sc[i]
        conv = convb_ref[...] + convw_ref[SSD_CONV - 1:SSD_CONV, :] * ext[CONV_ROWS:]
        for back in range(1, SSD_CONV):
            k = SSD_CONV - 1 - back
            conv = conv + convw_ref[k:k + 1, :] * pltpu.roll(ext, back, 0)[CONV_ROWS:]
        tail = ext_sc[i, rows:rows + CONV_ROWS, :]
        newconv_ref[i] = tail
        ext_sc[i, 0:CONV_ROWS, :] = tail

        act = conv * jax.nn.sigmoid(conv)
        xs = act[:, :SSD_DIM]
        bmat = act[:, SSD_DIM:SSD_DIM + LANES]
        cmat = act[:, SSD_DIM + LANES:]

        dtr = padded(dt_ref[i]) + dtb_ref[...]
        dt = jnp.maximum(dtr, 0.0) + jnp.log(1.0 + jnp.exp(-jnp.abs(dtr)))
        dt = jnp.where(row < rows, dt, 0.0)
        a_cs = dt * a_neg
        k = 1
        while k < q:
            a_cs = a_cs + jnp.where(row >= k, pltpu.roll(a_cs, k, 0), 0.0)
            k *= 2
        a_cs_t = a_cs.T
        a_last = a_cs[q - 1:q, :]
        decay_to_end = jnp.exp(a_last - a_cs)
        decay_from_start = jnp.exp(a_cs)
        chunk_decay = jnp.exp(a_last)

        dt_cols = _expand_heads(dt)
        xdt = xs * dt_cols
        xdec = xs * _expand_heads(dt * decay_to_end)

        bmat_b = bmat.astype(BF)
        cb = [_dot_nt(jnp.where((lane >= g * SSD_STATE) & (lane < (g + 1) * SSD_STATE), cmat, 0.0).astype(BF),
                      bmat_b) for g in range(SSD_GROUPS)]
        y = jnp.zeros((q, SSD_DIM), F32)
        for h in range(SSD_HEADS):
            seg = a_cs[:, h:h + 1] - a_cs_t[h:h + 1, :]
            lmat = jnp.exp(jnp.where(causal, seg, -jnp.inf))
            m = cb[h // (SSD_HEADS // SSD_GROUPS)] * lmat
            sel = (col384 >= h * SSD_HEADDIM) & (col384 < (h + 1) * SSD_HEADDIM)
            y = y + _dot(m.astype(BF), jnp.where(sel, xdt, 0.0).astype(BF))

        hbig = hbig_sc[i]
        y_off = _dot_nt(cmat.astype(BF), hbig.astype(BF)) * _expand_heads(decay_from_start)
        y = y + y_off + dskip_ref[...] * xs
        zf = padded(z_ref[i])
        gated = y * (zf * jax.nn.sigmoid(zf))
        y_ref[i] = _rms(gated, ng_ref[...])[:rows].astype(y_ref.dtype)

        new_states = _dot(xdec.T.astype(BF), bmat_b)
        cd_rows = _expand_heads(jnp.broadcast_to(chunk_decay, (q, LANES))).T[:, :LANES]
        hnew = hbig * cd_rows + jnp.where(state_valid, new_states, 0.0)
        hbig_sc[i] = hnew
        ht_ref[i] = jnp.where(upper_rows, pltpu.roll(hnew, SSD_STATE, 1), hnew)[:, :SSD_STATE]
        return carry

    if bb <= SSD_SEQ_UNROLL:
        for i in range(bb):
            per_seq(i, 0)
    else:
        lax.fori_loop(0, bb, per_seq, 0)


def _ssd(z, xbc, dt, hist, hbig0, convw, convb, dtb, alog, dskip, ng, bb, out_dtype):
    b, l, _ = z.shape
    rows = min(l, SSD_CHUNK)
    nc = l // rows
    seq = lambda i, c: (i, c, 0)
    fixed3 = lambda i, c: (i, 0, 0)
    par = lambda i, c: (0, 0)
    return pl.pallas_call(
        functools.partial(_ssd_kernel, rows=rows, bb=bb),
        grid=(b // bb, nc),
        in_specs=[pl.BlockSpec((bb, rows, SSD_DIM), seq),
                  pl.BlockSpec((bb, rows, SSD_CONV_DIM), seq),
                  pl.BlockSpec((bb, rows, LANES), seq),
                  pl.BlockSpec((bb, CONV_ROWS, SSD_CONV_DIM), fixed3),
                  pl.BlockSpec((bb, SSD_DIM, LANES), fixed3),
                  pl.BlockSpec((SSD_CONV, SSD_CONV_DIM), par),
                  pl.BlockSpec((1, SSD_CONV_DIM), par),
                  pl.BlockSpec((1, LANES), par),
                  pl.BlockSpec((1, LANES), par),
                  pl.BlockSpec((1, SSD_DIM), par),
                  pl.BlockSpec((1, SSD_DIM), par)],
        out_specs=[pl.BlockSpec((bb, rows, SSD_DIM), seq),
                   pl.BlockSpec((bb, CONV_ROWS, SSD_CONV_DIM), fixed3),
                   pl.BlockSpec((bb, SSD_DIM, SSD_STATE), fixed3)],
        out_shape=[jax.ShapeDtypeStruct((b, l, SSD_DIM), out_dtype),
                   jax.ShapeDtypeStruct((b, CONV_ROWS, SSD_CONV_DIM), F32),
                   jax.ShapeDtypeStruct((b, SSD_DIM, SSD_STATE), F32)],
        scratch_shapes=[pltpu.VMEM((bb, CONV_ROWS + SSD_CHUNK, SSD_CONV_DIM), F32),
                        pltpu.VMEM((bb, SSD_DIM, LANES), F32)],
        compiler_params=_cparams(2),
        name="ssd_mix",
    )(z, xbc, dt, hist, hbig0, convw, convb, dtb, alog, dskip, ng)


Q_NOPE_W = MLA_HEADS * MLA_NOPE
Q_PE_W = MLA_HEADS * LANES


def _mla_prep_kernel(cq_ref, ckv_ref, kpe_ref, kper_ref, cos_ref, sin_ref, qg_ref, kg_ref, wq_ref, wuk_ref,
                     q_ref, k_ref, ckvn_ref, kr_ref):
    hq = _rms(cq_ref[...], qg_ref[...]).astype(BF)
    q_nope = _dot(hq, wq_ref[:, :Q_NOPE_W])
    q_pe = _dot(hq, wq_ref[:, Q_NOPE_W:Q_NOPE_W + Q_PE_W])
    q_pe_rot = _dot(hq, wq_ref[:, Q_NOPE_W + Q_PE_W:])
    q_lat = _dot(q_nope.astype(BF), wuk_ref[...])
    cos = cos_ref[...]
    sin = sin_ref[...]
    for h in range(MLA_HEADS):
        sl = slice(h * LANES, (h + 1) * LANES)
        q_ref[h, :, 0:LANES] = q_lat[:, sl].astype(q_ref.dtype)
        q_ref[h, :, LANES:QK_WIDTH] = (q_pe[:, sl] * cos + q_pe_rot[:, sl] * sin).astype(q_ref.dtype)
    ckvn = _rms(ckv_ref[...], kg_ref[...])
    ckvn_ref[...] = ckvn
    kr = kpe_ref[...] * cos + kper_ref[...] * sin
    kr_ref[...] = kr[:, :MLA_ROPE]
    k_ref[:, 0:LANES] = ckvn.astype(k_ref.dtype)
    lane = lax.broadcasted_iota(jnp.int32, (1, LANES), 1)
    k_ref[:, LANES:QK_WIDTH] = jnp.where(lane == LANES - 1, 1.0, kr).astype(k_ref.dtype)


def _mla_prep(cq, ckv, kpe, kper, cos, sin, qg, kg, wq, wuk, qk_dtype):
    n = cq.shape[0]
    tm = _row_tile(n, 512)
    ntab = cos.shape[0] // tm
    rowblk = lambda i: (i, 0)
    tab = lambda i: (i % ntab, 0)
    par = lambda i: (0, 0)
    return pl.pallas_call(
        _mla_prep_kernel,
        grid=(n // tm,),
        in_specs=[pl.BlockSpec((tm, MLA_Q_LORA), rowblk),
                  pl.BlockSpec((tm, MLA_KV_LORA), rowblk),
                  pl.BlockSpec((tm, LANES), rowblk),
                  pl.BlockSpec((tm, LANES), rowblk),
                  pl.BlockSpec((tm, LANES), tab),
                  pl.BlockSpec((tm, LANES), tab),
                  pl.BlockSpec((1, MLA_Q_LORA), par),
                  pl.BlockSpec((1, MLA_KV_LORA), par),
                  pl.BlockSpec(wq.shape, par),
                  pl.BlockSpec(wuk.shape, par)],
        out_specs=[pl.BlockSpec((MLA_HEADS, tm, QK_WIDTH), lambda i: (0, i, 0)),
                   pl.BlockSpec((tm, QK_WIDTH), rowblk),
                   pl.BlockSpec((tm, MLA_KV_LORA), rowblk),
                   pl.BlockSpec((tm, MLA_ROPE), rowblk)],
        out_shape=[jax.ShapeDtypeStruct((MLA_HEADS, n, QK_WIDTH), qk_dtype),
                   jax.ShapeDtypeStruct((n, QK_WIDTH), qk_dtype),
                   jax.ShapeDtypeStruct((n, MLA_KV_LORA), F32),
                   jax.ShapeDtypeStruct((n, MLA_ROPE), F32)],
        compiler_params=_cparams(1),
        name="mla_prep",
    )(cq, ckv, kpe, kper, cos, sin, qg, kg, wq, wuk)


def _value_up_norm(o_lat, rows, wuv_ref, g_ref):
    out = jnp.zeros((rows, MLA_DIM), F32)
    for h in range(MLA_HEADS):
        out = out + _dot(o_lat[h * rows:(h + 1) * rows].astype(BF), wuv_ref[h])
    return _rms(out, g_ref[...])


def _attn_prompt_kernel(q_ref, k_ref, wuv_ref, g_ref, o_ref, m_sc, acc_sc, *, bq, bk, hu):
    i = pl.program_id(1)
    m_sc[...] = jnp.full(m_sc.shape, -jnp.inf, F32)
    acc_sc[...] = jnp.zeros(acc_sc.shape, F32)
    exp2_scale = MLA_SCALE * LOG2E

    def head_step(h, kb, mask):
        s = _dot_nt(q_ref[h], kb)
        if mask is not None:
            s = jnp.where(mask, s, -jnp.inf)
        m_old = m_sc[h]
        m_new = jnp.maximum(m_old, jnp.max(s, axis=1, keepdims=True))
        alpha = jnp.exp2((m_old - m_new) * exp2_scale)
        p = jnp.exp2((s - jnp.concatenate([m_new] * (s.shape[1] // LANES), axis=1)) * exp2_scale)
        acc_sc[h] = jnp.concatenate([alpha, alpha], axis=1) * acc_sc[h] + _dot(p.astype(BF), kb)
        m_sc[h] = m_new

    def kv_step(j, width, masked):
        start = pl.multiple_of(j * bk, bk)
        mask = None
        if masked:
            t_q = i * bq + lax.broadcasted_iota(jnp.int32, (bq, width), 0)
            t_k = start + lax.broadcasted_iota(jnp.int32, (bq, width), 1)
            mask = t_k <= t_q

        def head_group(gi, carry):
            kb = k_ref[pl.ds(start, width), :]
            for u in range(hu):
                head_step(gi * hu + u, kb, mask)
            return carry

        lax.fori_loop(0, MLA_HEADS // hu, head_group, 0)

    n_full = (i * bq) // bk

    def body(j, carry):
        kv_step(j, bk, False)
        return carry

    lax.fori_loop(0, n_full, body, 0)
    rest = (i + 1) * bq - n_full * bk
    for width in range(bq, bk + 1, bq):
        @pl.when(rest == width)
        def _():
            kv_step(n_full, width, True)
    out = jnp.zeros((bq, MLA_DIM), F32)
    for h in range(MLA_HEADS):
        acc = acc_sc[h]
        o_lat = acc[:, :MLA_KV_LORA] / acc[:, QK_WIDTH - 1:QK_WIDTH]
        out = out + _dot(o_lat.astype(BF), wuv_ref[h])
    o_ref[...] = _rms(out, g_ref[...]).astype(o_ref.dtype)


def _attn_prompt(q, k, wuv, g, batch, out_dtype):
    n = k.shape[0]
    l = n // batch
    bq = _row_tile(l, ATTN_BQ)
    bk = _row_tile(l, ATTN_BK)
    nq = l // bq
    return pl.pallas_call(
        functools.partial(_attn_prompt_kernel, bq=bq, bk=bk, hu=ATTN_HEAD_UNROLL),
        grid=(batch, nq),
        in_specs=[pl.BlockSpec((MLA_HEADS, bq, QK_WIDTH), lambda b, i: (0, b * nq + i, 0)),
                  pl.BlockSpec((l, QK_WIDTH), lambda b, i: (b, 0)),
                  pl.BlockSpec(wuv.shape, lambda b, i: (0, 0, 0)),
                  pl.BlockSpec((1, MLA_DIM), lambda b, i: (0, 0))],
        out_specs=pl.BlockSpec((bq, MLA_DIM), lambda b, i: (b * nq + i, 0)),
        out_shape=jax.ShapeDtypeStruct((n, MLA_DIM), out_dtype),
        scratch_shapes=[pltpu.VMEM((MLA_HEADS, bq, LANES), F32), pltpu.VMEM((MLA_HEADS, bq, QK_WIDTH), F32)],
        compiler_params=_cparams(2),
        name="attn_prompt",
    )(q, k, wuv, g)


def _attn_sample_kernel(pt_ref, q_ref, knew_ref, ckv_hbm, kr_hbm, wuv_ref, g_ref, o_ref,
                        kbuf, rbuf, sems, *, layer, n_pages, t_new):
    b = pl.program_id(0)
    nb = pl.num_programs(0)
    past = n_pages * PAGE_SIZE

    def page_copies(seq, slot, page_of):
        for j in range(n_pages):
            pg = page_of(seq, j)
            rows = pl.ds(j * PAGE_SIZE, PAGE_SIZE)
            yield pltpu.make_async_copy(ckv_hbm.at[layer, pg], kbuf.at[slot, rows, :], sems.at[0, slot])
            yield pltpu.make_async_copy(kr_hbm.at[layer, pg], rbuf.at[slot, :, rows], sems.at[1, slot])

    def start_fetch(seq, slot):
        for cp in page_copies(seq, slot, lambda s, j: pt_ref[s, j]):
            cp.start()

    def wait_fetch(slot):
        for cp in page_copies(0, slot, lambda s, j: 0):
            cp.wait()

    @pl.when(b == 0)
    def _():
        start_fetch(0, 0)

    @pl.when(b + 1 < nb)
    def _():
        start_fetch(b + 1, (b + 1) % 2)

    slot = b % 2
    wait_fetch(slot)

    rows = MLA_HEADS * t_new
    exp2_scale = MLA_SCALE * LOG2E
    q = q_ref[...].reshape(rows, QK_WIDTH).astype(BF)
    q_lat = q[:, :MLA_KV_LORA]
    q_pe = q[:, MLA_KV_LORA:MLA_KV_LORA + MLA_ROPE]

    kn = jnp.concatenate([knew_ref[...], jnp.zeros((LANES - t_new, QK_WIDTH), knew_ref.dtype)],
                         axis=0).astype(BF)
    sn = _dot_nt(q, kn)
    t_q = lax.broadcasted_iota(jnp.int32, (1, t_new, LANES), 1)
    t_k = lax.broadcasted_iota(jnp.int32, (1, t_new, LANES), 2)
    sn = jnp.where(t_k <= t_q, sn.reshape(MLA_HEADS, t_new, LANES), -jnp.inf).reshape(rows, LANES)
    kf = kbuf[slot].astype(BF)
    rf = rbuf[slot].astype(BF)
    s = _dot_nt(q_lat, kf) + _dot(q_pe, rf)
    m = jnp.maximum(jnp.max(s, axis=1, keepdims=True), jnp.max(sn, axis=1, keepdims=True))
    p = jnp.exp2((s - m) * exp2_scale)
    pn = jnp.exp2((sn - m) * exp2_scale)
    denom = jnp.sum(p, axis=1, keepdims=True) + jnp.sum(pn, axis=1, keepdims=True)
    o_lat = (_dot(p.astype(BF), kf) + _dot(pn.astype(BF), kn[:, :MLA_KV_LORA])) / denom
    o_ref[...] = _value_up_norm(o_lat, t_new, wuv_ref, g_ref).astype(o_ref.dtype)


def _attn_sample(page_table, q, knew, cache_kv, cache_kr, wuv, g, layer, out_dtype):
    nseq, n_pages = page_table.shape
    n = knew.shape[0]
    t_new = n // nseq
    past = n_pages * PAGE_SIZE
    grid_spec = pltpu.PrefetchScalarGridSpec(
        num_scalar_prefetch=1,
        grid=(nseq,),
        in_specs=[pl.BlockSpec((MLA_HEADS, t_new, QK_WIDTH), lambda b, pt: (0, b, 0)),
                  pl.BlockSpec((t_new, QK_WIDTH), lambda b, pt: (b, 0)),
                  pl.BlockSpec(memory_space=pl.ANY),
                  pl.BlockSpec(memory_space=pl.ANY),
                  pl.BlockSpec(wuv.shape, lambda b, pt: (0, 0, 0)),
                  pl.BlockSpec((1, MLA_DIM), lambda b, pt: (0, 0))],
        out_specs=pl.BlockSpec((t_new, MLA_DIM), lambda b, pt: (b, 0)),
        scratch_shapes=[pltpu.VMEM((2, past, MLA_KV_LORA), F32),
                        pltpu.VMEM((2, MLA_ROPE, past), F32),
                        pltpu.SemaphoreType.DMA((2, 2))],
    )
    return pl.pallas_call(
        functools.partial(_attn_sample_kernel, layer=layer, n_pages=n_pages, t_new=t_new),
        grid_spec=grid_spec,
        out_shape=jax.ShapeDtypeStruct((n, MLA_DIM), out_dtype),
        compiler_params=_cparams(1),
        name="attn_sample",
    )(page_table, q, knew, cache_kv, cache_kr, wuv, g)


def _out_ffn_kernel(x_ref, pool_ref, ssd_ref, mla_ref, wout_ref, g_ref, wup_ref, wdn_ref, gf_ref, o_ref,
                    xn_sc, hn_sc, acc_sc, *, final):
    k = pl.program_id(1)

    @pl.when(k == 0)
    def _():
        y = (_dot(pool_ref[...].astype(BF), wout_ref[0:POOL_DIM])
             + _dot(ssd_ref[...].astype(BF), wout_ref[POOL_DIM:POOL_DIM + SSD_DIM])
             + _dot(mla_ref[...].astype(BF), wout_ref[POOL_DIM + SSD_DIM:]))
        xn = x_ref[...] + y
        xn_sc[...] = xn
        hn_sc[...] = _rms(xn, g_ref[...]).astype(BF)
        acc_sc[...] = jnp.zeros_like(acc_sc)

    a = jnp.maximum(_dot(hn_sc[...], wup_ref[...]), 0.0)
    acc_sc[...] += _dot((a * a).astype(BF), wdn_ref[...])

    @pl.when(k == pl.num_programs(1) - 1)
    def _():
        x_out = xn_sc[...] + acc_sc[...]
        o_ref[...] = _rms(x_out, gf_ref[...]) if final else x_out


def _out_ffn(x, pool, ssd, mla, wout, g, wup, wdn, gfinal, final):
    n = x.shape[0]
    tm = _row_tile(n, FFN_ROWS)
    tf = FFN_COLS
    rowblk = lambda i, k: (i, 0)
    par = lambda i, k: (0, 0)
    return pl.pallas_call(
        functools.partial(_out_ffn_kernel, final=final),
        grid=(n // tm, FFN_DIM // tf),
        in_specs=[pl.BlockSpec((tm, D_MODEL), rowblk),
                  pl.BlockSpec((tm, POOL_DIM), rowblk),
                  pl.BlockSpec((tm, SSD_DIM), rowblk),
                  pl.BlockSpec((tm, MLA_DIM), rowblk),
                  pl.BlockSpec((D_MODEL, D_MODEL), par),
                  pl.BlockSpec((1, D_MODEL), par),
                  pl.BlockSpec((D_MODEL, tf), lambda i, k: (0, k)),
                  pl.BlockSpec((tf, D_MODEL), lambda i, k: (k, 0)),
                  pl.BlockSpec((1, D_MODEL), par)],
        out_specs=pl.BlockSpec((tm, D_MODEL), rowblk),
        out_shape=jax.ShapeDtypeStruct((n, D_MODEL), F32),
        scratch_shapes=[pltpu.VMEM((tm, D_MODEL), F32), pltpu.VMEM((tm, D_MODEL), BF),
                        pltpu.VMEM((tm, D_MODEL), F32)],
        compiler_params=_cparams(2),
        name="out_ffn",
    )(x, pool, ssd, mla, wout, g, wup, wdn, gfinal)


def _pad_cols(a, width):
    return jnp.pad(a, [(0, 0)] * (a.ndim - 1) + [(0, width - a.shape[-1])])


def _rot_half_cols(w):
    half = MLA_ROPE // 2
    return jnp.concatenate([-w[..., half:], w[..., :half]], axis=-1)


def _prep_weights(w_in, pool_w, w_uq, w_uk, w_uv, w_out, w_up, w_down):
    depth = w_in.shape[0]
    parts, start = [], 0
    for wd in IN_WIDTHS:
        parts.append(w_in[..., start:start + wd])
        start += wd
    u, z, xbc, dt, cq, ckv, kpe = parts
    w_in_p = jnp.concatenate([u, z, xbc, cq, ckv, _pad_cols(kpe, LANES), _pad_cols(_rot_half_cols(kpe), LANES),
                              _pad_cols(dt, LANES)], axis=-1).astype(BF)
    eye4 = jnp.eye(len(POOL_WINDOWS), dtype=F32)
    pool_bd = (pool_w[:, :, :, None, :] * eye4[None, :, None, :, None]).reshape(depth, POOL_DIM, POOL_DIM).astype(BF)
    wq = w_uq.reshape(depth, MLA_Q_LORA, MLA_HEADS, MLA_NOPE + MLA_ROPE)
    wq_nope = wq[..., :MLA_NOPE].reshape(depth, MLA_Q_LORA, Q_NOPE_W)
    wq_pe = wq[..., MLA_NOPE:]
    wq_p = jnp.concatenate([wq_nope,
                            _pad_cols(wq_pe, LANES).reshape(depth, MLA_Q_LORA, Q_PE_W),
                            _pad_cols(_rot_half_cols(wq_pe), LANES).reshape(depth, MLA_Q_LORA, Q_PE_W)],
                           axis=-1).astype(BF)
    eye6 = jnp.eye(MLA_HEADS, dtype=F32)
    uk = jnp.transpose(w_uk, (0, 2, 3, 1))
    wuk_bd = (uk[:, :, :, None, :] * eye6[None, :, None, :, None]).reshape(
        depth, Q_NOPE_W, MLA_HEADS * MLA_KV_LORA).astype(BF)
    uv = jnp.transpose(w_uv, (0, 2, 1, 3))
    wuv_pad = (uv[:, :, :, None, :] * eye6[None, :, None, :, None]).reshape(
        depth, MLA_HEADS, MLA_KV_LORA, MLA_DIM).astype(BF)
    return w_in_p, pool_bd, wq_p, wuk_bd, wuv_pad, w_out.astype(BF), w_up.astype(BF), w_down.astype(BF)


def _rope_tables(pos, reps):
    half = MLA_ROPE // 2
    inv = ROPE_THETA ** (-jnp.arange(half, dtype=F32) * (2.0 / MLA_ROPE))
    ang = pos.astype(F32)[:, None] * inv[None, :]
    cos = jnp.cos(ang)
    sin = jnp.sin(ang)
    cos = _pad_cols(jnp.concatenate([cos, cos], axis=-1), LANES)
    sin = _pad_cols(jnp.concatenate([sin, sin], axis=-1), LANES)
    return jnp.tile(cos, (reps, 1)), jnp.tile(sin, (reps, 1))


def _state_to_rows(h):
    return _pad_cols(h.reshape(h.shape[0], SSD_DIM, SSD_STATE), LANES)


def _group_layer(x, batch, lw, pos0, pool_hist, conv_hist, ssm_h0, tables, attend, bb, act_dtype):
    n = x.shape[0]
    l = n // batch
    u, z, xbc, cq, ckv, kpe, kper, dt = _in_proj(x, lw["norm_mix_g"], lw["w_in"])

    pool_out, pool_tail = _pool(u.reshape(batch, l, POOL_DIM), pool_hist, lw["pool_bd"], lw["pool_scale"],
                                pos0, act_dtype)
    ssd_out, conv_tail, hbig = _ssd(z.reshape(batch, l, SSD_DIM), xbc.reshape(batch, l, SSD_CONV_DIM),
                                    dt.reshape(batch, l, LANES), conv_hist, ssm_h0,
                                    lw["conv_w"], lw["conv_b"], lw["dt_bias"], lw["a_log"], lw["d_skip"],
                                    lw["ssd_norm_g"], bb, act_dtype)
    q, kcat, ckv_n, kpe_r = _mla_prep(cq, ckv, kpe, kper, tables[0], tables[1], lw["q_norm_g"], lw["kv_norm_g"],
                                      lw["wq"], lw["wuk_bd"], act_dtype)
    mla_out = attend(q, kcat)
    x_new = _out_ffn(x, pool_out.reshape(n, POOL_DIM), ssd_out.reshape(n, SSD_DIM), mla_out,
                     lw["w_out"], lw["norm_ffn_g"], lw["w_up"], lw["w_down"], lw["final_g"], lw["final"])
    new_pool = pool_tail[:, HIST_ROWS - POOL_HIST:]
    new_conv = conv_tail[:, CONV_ROWS - (SSD_CONV - 1):]
    new_ssm = hbig.reshape(batch, SSD_HEADS, SSD_HEADDIM, SSD_STATE)
    return (x_new, ckv_n.reshape(batch, l, MLA_KV_LORA), kpe_r.reshape(batch, l, MLA_ROPE),
            new_pool, new_conv, new_ssm)


def kernel(x_prompt, x_sample, cache_kv_latent, cache_k_rope, state_pool, state_conv, state_ssm, page_table,
           norm_mix_g, w_in, pool_w, pool_scale, conv_w, conv_b, dt_bias, a_log, d_skip, ssd_norm_g, q_norm_g,
           w_uq, kv_norm_g, w_uk, w_uv, mla_out_g, w_out, norm_ffn_g, w_up, w_down, final_norm_g):
    depth = w_in.shape[0]
    bp, seq, _ = x_prompt.shape
    db, dec_seq, _ = x_sample.shape
    past_len = page_table.shape[1] * PAGE_SIZE

    w_in_p, pool_bd, wq_p, wuk_bd, wuv_pad, w_out_b, w_up_b, w_down_b = _prep_weights(
        w_in, pool_w, w_uq, w_uk, w_uv, w_out, w_up, w_down)
    tables_p = _rope_tables(jnp.arange(seq, dtype=jnp.int32), 1)
    tile_s = _row_tile(db * dec_seq, 512)
    tables_s = _rope_tables(past_len + jnp.arange(dec_seq, dtype=jnp.int32), tile_s // dec_seq)

    cache_kr_t = jnp.swapaxes(cache_k_rope, 2, 3)

    xp = x_prompt.reshape(bp * seq, D_MODEL)
    xs = x_sample.reshape(db * dec_seq, D_MODEL)
    zero_pool = jnp.zeros((bp, HIST_ROWS, POOL_DIM), F32)
    zero_conv = jnp.zeros((bp, CONV_ROWS, SSD_CONV_DIM), F32)
    zero_ssm = jnp.zeros((bp, SSD_DIM, LANES), F32)
    outs_p, outs_s = [], []
    for l in range(depth):
        lw = {
            "norm_mix_g": norm_mix_g[l][None], "w_in": w_in_p[l], "pool_bd": pool_bd[l],
            "pool_scale": pool_scale[l][None], "conv_w": conv_w[l], "conv_b": conv_b[l][None],
            "dt_bias": _pad_cols(dt_bias[l][None], LANES), "a_log": _pad_cols(a_log[l][None], LANES),
            "d_skip": jnp.repeat(d_skip[l], SSD_HEADDIM)[None], "ssd_norm_g": ssd_norm_g[l][None],
            "q_norm_g": q_norm_g[l][None], "kv_norm_g": kv_norm_g[l][None], "wq": wq_p[l], "wuk_bd": wuk_bd[l],
            "w_out": w_out_b[l], "norm_ffn_g": norm_ffn_g[l][None], "w_up": w_up_b[l], "w_down": w_down_b[l],
            "final_g": final_norm_g[None], "final": l == depth - 1,
        }
        wuv_l = wuv_pad[l]
        og = mla_out_g[l][None]

        attend_p = lambda q, k: _attn_prompt(q, k, wuv_l, og, bp, BF)
        res = _group_layer(xp, bp, lw, 0, zero_pool, zero_conv, zero_ssm, tables_p, attend_p, min(bp, SSD_SEQ_UNROLL), BF)
        xp = res[0]
        outs_p.append(res[1:])

        attend_s = lambda q, k: _attn_sample(page_table, q, k, cache_kv_latent, cache_kr_t, wuv_l, og, l, F32)
        pool_hist = jnp.pad(state_pool[l], ((0, 0), (HIST_ROWS - POOL_HIST, 0), (0, 0)))
        conv_hist = jnp.pad(state_conv[l], ((0, 0), (CONV_ROWS - (SSD_CONV - 1), 0), (0, 0)))
        res = _group_layer(xs, db, lw, past_len, pool_hist, conv_hist, _state_to_rows(state_ssm[l]),
                           tables_s, attend_s, 8, F32)
        xs = res[0]
        outs_s.append(res[1:])

    y_prompt = xp.reshape(bp, seq, D_MODEL)
    y_sample = xs.reshape(db, dec_seq, D_MODEL)
    stack = lambda outs, k: jnp.stack([o[k] for o in outs])
    return (y_prompt, y_sample,
            stack(outs_p, 0), stack(outs_p, 1), stack(outs_p, 2), stack(outs_p, 3), stack(outs_p, 4),
            stack(outs_s, 0), stack(outs_s, 1), stack(outs_s, 2), stack(outs_s, 3), stack(outs_s, 4))
```

```python
import functools

import jax
import jax.numpy as jnp
from jax import lax
from jax.experimental import pallas as pl
from jax.experimental.pallas import tpu as pltpu

D_MODEL = 1024
POOL_WINDOWS = (2, 4, 8, 16)
POOL_DIM = 256
POOL_GROUP = 64
POOL_HIST = 15
SSD_HEADDIM = 64
SSD_HEADS = 6
SSD_DIM = 384
SSD_GROUPS = 2
SSD_STATE = 64
SSD_CONV = 4
SSD_CONV_DIM = 640
SSD_CHUNK = 128
MLA_HEADS = 6
MLA_NOPE = 64
MLA_ROPE = 32
MLA_V = 64
MLA_Q_LORA = 256
MLA_KV_LORA = 128
MLA_DIM = 384
MLA_SCALE = (MLA_NOPE + MLA_ROPE) ** -0.5
ROPE_THETA = 10000.0
FFN_DIM = 4096
NORM_EPS = 1e-6
PAGE_SIZE = 128
IN_WIDTHS = (POOL_DIM, SSD_DIM, SSD_CONV_DIM, SSD_HEADS, MLA_Q_LORA, MLA_KV_LORA, MLA_ROPE)

LANES = 128
SUBLANES = 8
HIST_ROWS = 16
CONV_ROWS = 8
QK_WIDTH = 2 * LANES
VMEM_LIMIT = 48 * 1024 * 1024
ATTN_BQ = 512
ATTN_BK = 512
ATTN_HEAD_UNROLL = 6
LOG2E = 1.4426950408889634
POOL_TILE = 1024
FFN_ROWS = 1024
FFN_COLS = 1024
SSD_SEQ_UNROLL = 1

BF = jnp.bfloat16
F32 = jnp.float32


def _cparams(n_axes):
    return pltpu.CompilerParams(dimension_semantics=("arbitrary",) * n_axes,
                                vmem_limit_bytes=VMEM_LIMIT)


def _rms(x, g):
    return x * lax.rsqrt(jnp.mean(x * x, axis=-1, keepdims=True) + NORM_EPS) * g


def _dot(a, b):
    return jnp.dot(a, b, preferred_element_type=F32)


def _dot_nt(a, b):
    return lax.dot_general(a, b, (((1,), (1,)), ((), ())), preferred_element_type=F32)


def _row_tile(n, target):
    t = min(n, target)
    while n % t:
        t //= 2
    return t


IN_OUT_WIDTHS = (POOL_DIM, SSD_DIM, SSD_CONV_DIM, MLA_Q_LORA, MLA_KV_LORA, LANES, LANES, LANES)


def _in_proj_kernel(x_ref, g_ref, w_ref, *out_refs):
    h = _rms(x_ref[...], g_ref[...]).astype(BF)
    proj = _dot(h, w_ref[...])
    off = 0
    for o_ref in out_refs:
        width = o_ref.shape[-1]
        o_ref[...] = proj[:, off:off + width]
        off += width


def _in_proj(x, g, w):
    n = x.shape[0]
    tm = _row_tile(n, 512)
    wtot = w.shape[1]
    return pl.pallas_call(
        _in_proj_kernel,
        grid=(n // tm,),
        in_specs=[pl.BlockSpec((tm, D_MODEL), lambda i: (i, 0)),
                  pl.BlockSpec((1, D_MODEL), lambda i: (0, 0)),
                  pl.BlockSpec((D_MODEL, wtot), lambda i: (0, 0))],
        out_specs=[pl.BlockSpec((tm, wd), lambda i: (i, 0)) for wd in IN_OUT_WIDTHS],
        out_shape=[jax.ShapeDtypeStruct((n, wd), F32) for wd in IN_OUT_WIDTHS],
        compiler_params=_cparams(1),
        name="in_proj",
    )(x, g, w)


def _pool_kernel(u_ref, hist_ref, w_ref, scale_ref, out_ref, newhist_ref, carry_sc, *, pos0, tile, bb):
    t = pl.program_id(1)
    lane = lax.broadcasted_iota(jnp.int32, (1, POOL_DIM), 1)
    g0, g1, g2 = lane < POOL_GROUP, lane < 2 * POOL_GROUP, lane < 3 * POOL_GROUP
    width = jnp.where(g0, 2, jnp.where(g1, 4, jnp.where(g2, 8, 16)))
    pos = pos0 + t * tile + lax.broadcasted_iota(jnp.int32, (tile, 1), 0)
    cnt = jnp.minimum(width, pos + 1).astype(F32)

    def per_seq(i, carry):
        @pl.when(t == 0)
        def _():
            carry_sc[i] = hist_ref[i]

        u = u_ref[i]
        ext = jnp.concatenate([carry_sc[i], u], axis=0)
        s2 = ext + pltpu.roll(ext, 1, 0)
        s4 = s2 + pltpu.roll(s2, 2, 0)
        s8 = s4 + pltpu.roll(s4, 4, 0)
        s16 = s8 + pltpu.roll(s8, 8, 0)
        wsum = jnp.where(g0, s2, jnp.where(g1, s4, jnp.where(g2, s8, s16)))[HIST_ROWS:]
        m = wsum / cnt - u
        out = _dot(m.astype(BF), w_ref[...]) * scale_ref[...]
        out_ref[i] = out.astype(out_ref.dtype)
        last = ext[tile:tile + HIST_ROWS]
        carry_sc[i] = last
        newhist_ref[i] = last
        return carry

    if bb == 1:
        per_seq(0, 0)
    else:
        lax.fori_loop(0, bb, per_seq, 0)


def _pool(u, hist, wbd, scale, pos0, out_dtype):
    b, l, _ = u.shape
    tile = _row_tile(l, POOL_TILE)
    bb = _row_tile(b, max(1, POOL_TILE // l))
    return pl.pallas_call(
        functools.partial(_pool_kernel, pos0=pos0, tile=tile, bb=bb),
        grid=(b // bb, l // tile),
        in_specs=[pl.BlockSpec((bb, tile, POOL_DIM), lambda i, t: (i, t, 0)),
                  pl.BlockSpec((bb, HIST_ROWS, POOL_DIM), lambda i, t: (i, 0, 0)),
                  pl.BlockSpec((POOL_DIM, POOL_DIM), lambda i, t: (0, 0)),
                  pl.BlockSpec((1, POOL_DIM), lambda i, t: (0, 0))],
        out_specs=[pl.BlockSpec((bb, tile, POOL_DIM), lambda i, t: (i, t, 0)),
                   pl.BlockSpec((bb, HIST_ROWS, POOL_DIM), lambda i, t: (i, 0, 0))],
        out_shape=[jax.ShapeDtypeStruct((b, l, POOL_DIM), out_dtype),
                   jax.ShapeDtypeStruct((b, HIST_ROWS, POOL_DIM), F32)],
        scratch_shapes=[pltpu.VMEM((bb, HIST_ROWS, POOL_DIM), F32)],
        compiler_params=_cparams(2),
        name="pool_mix",
    )(u, hist, wbd, scale)


def _expand_heads(x):
    q = x.shape[0]
    col = lax.broadcasted_iota(jnp.int32, (1, SSD_DIM), 1)
    out = jnp.zeros((q, SSD_DIM), F32)
    for h in range(SSD_HEADS):
        sel = (col >= h * SSD_HEADDIM) & (col < (h + 1) * SSD_HEADDIM)
        out = jnp.where(sel, jnp.broadcast_to(x[:, h:h + 1], (q, SSD_DIM)), out)
    return out


def _ssd_kernel(z_ref, xbc_ref, dt_ref, hist_ref, h0_ref, convw_ref, convb_ref, dtb_ref, alog_ref,
                dskip_ref, ng_ref, y_ref, newconv_ref, ht_ref, ext_sc, hbig_sc, *, rows, bb):
    c = pl.program_id(1)
    q = SSD_CHUNK
    pad = q - rows

    def padded(x):
        if pad == 0:
            return x
        return jnp.concatenate([x, jnp.zeros((pad, x.shape[1]), x.dtype)], axis=0)

    row = lax.broadcasted_iota(jnp.int32, (q, 1), 0)
    col_t = lax.broadcasted_iota(jnp.int32, (1, q), 1)
    causal = row >= col_t
    lane = lax.broadcasted_iota(jnp.int32, (1, LANES), 1)
    row384 = lax.broadcasted_iota(jnp.int32, (SSD_DIM, 1), 0)
    upper_rows = row384 >= SSD_DIM // SSD_GROUPS
    state_valid = jnp.where(upper_rows, 1, 0) == jnp.where(lane >= SSD_STATE, 1, 0)
    a_neg = -jnp.exp(alog_ref[...])

    def per_seq(i, carry):
        @pl.when(c == 0)
        def _():
            ext_sc[i, 0:CONV_ROWS, :] = hist_ref[i]
            hbig_sc[i] = jnp.zeros((SSD_DIM, LANES), F32)
            hbig_sc[i, :, 0:SSD_STATE] = h0_ref[i]
            h0 = hbig_sc[i]
            hbig_sc[i] = jnp.where(upper_rows, pltpu.roll(h0, SSD_STATE, 1), h0)

        ext_sc[i, CONV_ROWS:CONV_ROWS + q, :] = padded(xbc_ref[i])
        ext = ext_sc[i]
        conv = convb_ref[...] + convw_ref[SSD_CONV - 1:SSD_CONV, :] * ext[CONV_ROWS:]
        for back in range(1, SSD_CONV):
            k = SSD_CONV - 1 - back
            conv = conv + convw_ref[k:k + 1, :] * pltpu.roll(ext, back, 0)[CONV_ROWS:]
        tail = ext_sc[i, rows:rows + CONV_ROWS, :]
        newconv_ref[i] = tail
        ext_sc[i, 0:CONV_ROWS, :] = tail

        act = conv * jax.nn.sigmoid(conv)
        xs = act[:, :SSD_DIM]
        bmat = act[:, SSD_DIM:SSD_DIM + LANES]
        cmat = act[:, SSD_DIM + LANES:]

        dtr = padded(dt_ref[i]) + dtb_ref[...]
        dt = jnp.maximum(dtr, 0.0) + jnp.log(1.0 + jnp.exp(-jnp.abs(dtr)))
        dt = jnp.where(row < rows, dt, 0.0)
        a_cs = dt * a_neg
        k = 1
        while k < q:
            a_cs = a_cs + jnp.where(row >= k, pltpu.roll(a_cs, k, 0), 0.0)
            k *= 2
        a_cs_t = a_cs.T
        a_last = a_cs[q - 1:q, :]
        decay_to_end = jnp.exp(a_last - a_cs)
        decay_from_start = jnp.exp(a_cs)
        chunk_decay = jnp.exp(a_last)

        dt_t = dt.T
        xdec = xs * _expand_heads(dt * decay_to_end)

        bmat_b = bmat.astype(BF)
        cb = [_dot_nt(jnp.where((lane >= g * SSD_STATE) & (lane < (g + 1) * SSD_STATE), cmat, 0.0).astype(BF),
                      bmat_b) for g in range(SSD_GROUPS)]
        tiles = []
        for t in range(SSD_DIM // LANES):
            xt = xs[:, t * LANES:(t + 1) * LANES]
            yt = jnp.zeros((q, LANES), F32)
            for h in (2 * t, 2 * t + 1):
                seg = a_cs[:, h:h + 1] - a_cs_t[h:h + 1, :]
                lmat = jnp.exp(jnp.where(causal, seg, -jnp.inf))
                m = cb[h // (SSD_HEADS // SSD_GROUPS)] * lmat * dt_t[h:h + 1, :]
                own = (lane < SSD_HEADDIM) if h % 2 == 0 else (lane >= SSD_HEADDIM)
                yt = yt + _dot(m.astype(BF), jnp.where(own, xt, 0.0).astype(BF))
            tiles.append(yt)
        y = jnp.concatenate(tiles, axis=1)

        hbig = hbig_sc[i]
        y_off = _dot_nt(cmat.astype(BF), hbig.astype(BF)) * _expand_heads(decay_from_start)
        y = y + y_off + dskip_ref[...] * xs
        zf = padded(z_ref[i])
        gated = y * (zf * jax.nn.sigmoid(zf))
        y_ref[i] = _rms(gated, ng_ref[...])[:rows].astype(y_ref.dtype)

        new_states = _dot(xdec.T.astype(BF), bmat_b)
        cd_rows = _expand_heads(jnp.broadcast_to(chunk_decay, (q, LANES))).T[:, :LANES]
        hnew = hbig * cd_rows + jnp.where(state_valid, new_states, 0.0)
        hbig_sc[i] = hnew
        ht_ref[i] = jnp.where(upper_rows, pltpu.roll(hnew, SSD_STATE, 1), hnew)[:, :SSD_STATE]
        return carry

    if bb <= SSD_SEQ_UNROLL:
        for i in range(bb):
            per_seq(i, 0)
    else:
        lax.fori_loop(0, bb, per_seq, 0)


def _ssd(z, xbc, dt, hist, hbig0, convw, convb, dtb, alog, dskip, ng, bb, out_dtype):
    b, l, _ = z.shape
    rows = min(l, SSD_CHUNK)
    nc = l // rows
    seq = lambda i, c: (i, c, 0)
    fixed3 = lambda i, c: (i, 0, 0)
    par = lambda i, c: (0, 0)
    return pl.pallas_call(
        functools.partial(_ssd_kernel, rows=rows, bb=bb),
        grid=(b // bb, nc),
        in_specs=[pl.BlockSpec((bb, rows, SSD_DIM), seq),
                  pl.BlockSpec((bb, rows, SSD_CONV_DIM), seq),
                  pl.BlockSpec((bb, rows, LANES), seq),
                  pl.BlockSpec((bb, CONV_ROWS, SSD_CONV_DIM), fixed3),
                  pl.BlockSpec((bb, SSD_DIM, SSD_STATE), fixed3),
                  pl.BlockSpec((SSD_CONV, SSD_CONV_DIM), par),
                  pl.BlockSpec((1, SSD_CONV_DIM), par),
                  pl.BlockSpec((1, LANES), par),
                  pl.BlockSpec((1, LANES), par),
                  pl.BlockSpec((1, SSD_DIM), par),
                  pl.BlockSpec((1, SSD_DIM), par)],
        out_specs=[pl.BlockSpec((bb, rows, SSD_DIM), seq),
                   pl.BlockSpec((bb, CONV_ROWS, SSD_CONV_DIM), fixed3),
                   pl.BlockSpec((bb, SSD_DIM, SSD_STATE), fixed3)],
        out_shape=[jax.ShapeDtypeStruct((b, l, SSD_DIM), out_dtype),
                   jax.ShapeDtypeStruct((b, CONV_ROWS, SSD_CONV_DIM), F32),
                   jax.ShapeDtypeStruct((b, SSD_DIM, SSD_STATE), F32)],
        scratch_shapes=[pltpu.VMEM((bb, CONV_ROWS + SSD_CHUNK, SSD_CONV_DIM), F32),
                        pltpu.VMEM((bb, SSD_DIM, LANES), F32)],
        compiler_params=_cparams(2),
        name="ssd_mix",
    )(z, xbc, dt, hist, hbig0, convw, convb, dtb, alog, dskip, ng)


Q_NOPE_W = MLA_HEADS * MLA_NOPE
Q_PE_W = MLA_HEADS * LANES


def _mla_prep_kernel(cq_ref, ckv_ref, kpe_ref, kper_ref, cos_ref, sin_ref, qg_ref, kg_ref, wq_ref, wuk_ref,
                     q_ref, k_ref, ckvn_ref, kr_ref):
    hq = _rms(cq_ref[...], qg_ref[...]).astype(BF)
    q_nope = _dot(hq, wq_ref[:, :Q_NOPE_W])
    q_pe = _dot(hq, wq_ref[:, Q_NOPE_W:Q_NOPE_W + Q_PE_W])
    q_pe_rot = _dot(hq, wq_ref[:, Q_NOPE_W + Q_PE_W:])
    q_lat = _dot(q_nope.astype(BF), wuk_ref[...])
    cos = cos_ref[...]
    sin = sin_ref[...]
    for h in range(MLA_HEADS):
        sl = slice(h * LANES, (h + 1) * LANES)
        q_ref[h, :, 0:LANES] = q_lat[:, sl].astype(q_ref.dtype)
        q_ref[h, :, LANES:QK_WIDTH] = (q_pe[:, sl] * cos + q_pe_rot[:, sl] * sin).astype(q_ref.dtype)
    ckvn = _rms(ckv_ref[...], kg_ref[...])
    ckvn_ref[...] = ckvn
    kr = kpe_ref[...] * cos + kper_ref[...] * sin
    kr_ref[...] = kr[:, :MLA_ROPE]
    k_ref[:, 0:LANES] = ckvn.astype(k_ref.dtype)
    lane = lax.broadcasted_iota(jnp.int32, (1, LANES), 1)
    k_ref[:, LANES:QK_WIDTH] = jnp.where(lane == LANES - 1, 1.0, kr).astype(k_ref.dtype)


def _mla_prep(cq, ckv, kpe, kper, cos, sin, qg, kg, wq, wuk, qk_dtype):
    n = cq.shape[0]
    tm = _row_tile(n, 512)
    ntab = cos.shape[0] // tm
    rowblk = lambda i: (i, 0)
    tab = lambda i: (i % ntab, 0)
    par = lambda i: (0, 0)
    return pl.pallas_call(
        _mla_prep_kernel,
        grid=(n // tm,),
        in_specs=[pl.BlockSpec((tm, MLA_Q_LORA), rowblk),
                  pl.BlockSpec((tm, MLA_KV_LORA), rowblk),
                  pl.BlockSpec((tm, LANES), rowblk),
                  pl.BlockSpec((tm, LANES), rowblk),
                  pl.BlockSpec((tm, LANES), tab),
                  pl.BlockSpec((tm, LANES), tab),
                  pl.BlockSpec((1, MLA_Q_LORA), par),
                  pl.BlockSpec((1, MLA_KV_LORA), par),
                  pl.BlockSpec(wq.shape, par),
                  pl.BlockSpec(wuk.shape, par)],
        out_specs=[pl.BlockSpec((MLA_HEADS, tm, QK_WIDTH), lambda i: (0, i, 0)),
                   pl.BlockSpec((tm, QK_WIDTH), rowblk),
                   pl.BlockSpec((tm, MLA_KV_LORA), rowblk),
                   pl.BlockSpec((tm, MLA_ROPE), rowblk)],
        out_shape=[jax.ShapeDtypeStruct((MLA_HEADS, n, QK_WIDTH), qk_dtype),
                   jax.ShapeDtypeStruct((n, QK_WIDTH), qk_dtype),
                   jax.ShapeDtypeStruct((n, MLA_KV_LORA), F32),
                   jax.ShapeDtypeStruct((n, MLA_ROPE), F32)],
        compiler_params=_cparams(1),
        name="mla_prep",
    )(cq, ckv, kpe, kper, cos, sin, qg, kg, wq, wuk)


def _value_up_norm(o_lat, rows, wuv_ref, g_ref):
    out = jnp.zeros((rows, MLA_DIM), F32)
    for h in range(MLA_HEADS):
        out = out + _dot(o_lat[h * rows:(h + 1) * rows].astype(BF), wuv_ref[h])
    return _rms(out, g_ref[...])


def _attn_prompt_kernel(q_ref, k_ref, wuv_ref, g_ref, o_ref, m_sc, acc_sc, *, bq, bk, hu):
    i = pl.program_id(1)
    m_sc[...] = jnp.full(m_sc.shape, -jnp.inf, F32)
    acc_sc[...] = jnp.zeros(acc_sc.shape, F32)
    exp2_scale = MLA_SCALE * LOG2E

    def head_step(h, kb, mask):
        s = _dot_nt(q_ref[h], kb)
        if mask is not None:
            s = jnp.where(mask, s, -jnp.inf)
        m_old = m_sc[h]
        m_new = jnp.maximum(m_old, jnp.max(s, axis=1, keepdims=True))
        alpha = jnp.exp2((m_old - m_new) * exp2_scale)
        p = jnp.exp2((s - jnp.concatenate([m_new] * (s.shape[1] // LANES), axis=1)) * exp2_scale)
        acc_sc[h] = jnp.concatenate([alpha, alpha], axis=1) * acc_sc[h] + _dot(p.astype(BF), kb)
        m_sc[h] = m_new

    def kv_step(j, width, masked):
        start = pl.multiple_of(j * bk, bk)
        mask = None
        if masked:
            t_q = i * bq + lax.broadcasted_iota(jnp.int32, (bq, width), 0)
            t_k = start + lax.broadcasted_iota(jnp.int32, (bq, width), 1)
            mask = t_k <= t_q

        def head_group(gi, carry):
            kb = k_ref[pl.ds(start, width), :]
            for u in range(hu):
                head_step(gi * hu + u, kb, mask)
            return carry

        lax.fori_loop(0, MLA_HEADS // hu, head_group, 0)

    n_full = (i * bq) // bk

    def body(j, carry):
        kv_step(j, bk, False)
        return carry

    lax.fori_loop(0, n_full, body, 0)
    rest = (i + 1) * bq - n_full * bk
    for width in range(bq, bk + 1, bq):
        @pl.when(rest == width)
        def _():
            kv_step(n_full, width, True)
    out = jnp.zeros((bq, MLA_DIM), F32)
    for h in range(MLA_HEADS):
        acc = acc_sc[h]
        o_lat = acc[:, :MLA_KV_LORA] / acc[:, QK_WIDTH - 1:QK_WIDTH]
        out = out + _dot(o_lat.astype(BF), wuv_ref[h])
    o_ref[...] = _rms(out, g_ref[...]).astype(o_ref.dtype)


def _attn_prompt(q, k, wuv, g, batch, out_dtype):
    n = k.shape[0]
    l = n // batch
    bq = _row_tile(l, ATTN_BQ)
    bk = _row_tile(l, ATTN_BK)
    nq = l // bq
    return pl.pallas_call(
        functools.partial(_attn_prompt_kernel, bq=bq, bk=bk, hu=ATTN_HEAD_UNROLL),
        grid=(batch, nq),
        in_specs=[pl.BlockSpec((MLA_HEADS, bq, QK_WIDTH), lambda b, i: (0, b * nq + i, 0)),
                  pl.BlockSpec((l, QK_WIDTH), lambda b, i: (b, 0)),
                  pl.BlockSpec(wuv.shape, lambda b, i: (0, 0, 0)),
                  pl.BlockSpec((1, MLA_DIM), lambda b, i: (0, 0))],
        out_specs=pl.BlockSpec((bq, MLA_DIM), lambda b, i: (b * nq + i, 0)),
        out_shape=jax.ShapeDtypeStruct((n, MLA_DIM), out_dtype),
        scratch_shapes=[pltpu.VMEM((MLA_HEADS, bq, LANES), F32), pltpu.VMEM((MLA_HEADS, bq, QK_WIDTH), F32)],
        compiler_params=_cparams(2),
        name="attn_prompt",
    )(q, k, wuv, g)


def _attn_sample_kernel(pt_ref, q_ref, knew_ref, ckv_hbm, kr_hbm, wuv_ref, g_ref, o_ref,
                        kbuf, rbuf, sems, *, layer, n_pages, t_new):
    b = pl.program_id(0)
    nb = pl.num_programs(0)
    past = n_pages * PAGE_SIZE

    def start_fetch(seq, slot):
        for j in range(n_pages):
            pg = pt_ref[seq, j]
            rows = pl.ds(j * PAGE_SIZE, PAGE_SIZE)
            pltpu.make_async_copy(ckv_hbm.at[layer, pg], kbuf.at[slot, rows, :], sems.at[0, slot]).start()
            pltpu.make_async_copy(kr_hbm.at[layer, pg], rbuf.at[slot, :, rows], sems.at[1, slot]).start()

    def wait_fetch(slot):
        pltpu.make_async_copy(kbuf.at[1 - slot], kbuf.at[slot], sems.at[0, slot]).wait()
        pltpu.make_async_copy(rbuf.at[1 - slot], rbuf.at[slot], sems.at[1, slot]).wait()

    @pl.when(b == 0)
    def _():
        start_fetch(0, 0)

    @pl.when(b + 1 < nb)
    def _():
        start_fetch(b + 1, (b + 1) % 2)

    slot = b % 2
    wait_fetch(slot)

    rows = MLA_HEADS * t_new
    exp2_scale = MLA_SCALE * LOG2E
    q = q_ref[...].reshape(rows, QK_WIDTH).astype(BF)
    q_lat = q[:, :MLA_KV_LORA]
    q_pe = q[:, MLA_KV_LORA:MLA_KV_LORA + MLA_ROPE]

    kn = jnp.concatenate([knew_ref[...], jnp.zeros((LANES - t_new, QK_WIDTH), knew_ref.dtype)],
                         axis=0).astype(BF)
    sn = _dot_nt(q, kn)
    t_q = lax.broadcasted_iota(jnp.int32, (1, t_new, LANES), 1)
    t_k = lax.broadcasted_iota(jnp.int32, (1, t_new, LANES), 2)
    sn = jnp.where(t_k <= t_q, sn.reshape(MLA_HEADS, t_new, LANES), -jnp.inf).reshape(rows, LANES)
    kf = kbuf[slot].astype(BF)
    rf = rbuf[slot].astype(BF)
    s = _dot_nt(q_lat, kf) + _dot(q_pe, rf)
    m = jnp.maximum(jnp.max(s, axis=1, keepdims=True), jnp.max(sn, axis=1, keepdims=True))
    p = jnp.exp2((s - m) * exp2_scale)
    pn = jnp.exp2((sn - m) * exp2_scale)
    denom = jnp.sum(p, axis=1, keepdims=True) + jnp.sum(pn, axis=1, keepdims=True)
    o_lat = (_dot(p.astype(BF), kf) + _dot(pn.astype(BF), kn[:, :MLA_KV_LORA])) / denom
    o_ref[...] = _value_up_norm(o_lat, t_new, wuv_ref, g_ref).astype(o_ref.dtype)


def _attn_sample(page_table, q, knew, cache_kv, cache_kr, wuv, g, layer, out_dtype):
    nseq, n_pages = page_table.shape
    n = knew.shape[0]
    t_new = n // nseq
    past = n_pages * PAGE_SIZE
    grid_spec = pltpu.PrefetchScalarGridSpec(
        num_scalar_prefetch=1,
        grid=(nseq,),
        in_specs=[pl.BlockSpec((MLA_HEADS, t_new, QK_WIDTH), lambda b, pt: (0, b, 0)),
                  pl.BlockSpec((t_new, QK_WIDTH), lambda b, pt: (b, 0)),
                  pl.BlockSpec(memory_space=pl.ANY),
                  pl.BlockSpec(memory_space=pl.ANY),
                  pl.BlockSpec(wuv.shape, lambda b, pt: (0, 0, 0)),
                  pl.BlockSpec((1, MLA_DIM), lambda b, pt: (0, 0))],
        out_specs=pl.BlockSpec((t_new, MLA_DIM), lambda b, pt: (b, 0)),
        scratch_shapes=[pltpu.VMEM((2, past, MLA_KV_LORA), F32),
                        pltpu.VMEM((2, MLA_ROPE, past), F32),
                        pltpu.SemaphoreType.DMA((2, 2))],
    )
    return pl.pallas_call(
        functools.partial(_attn_sample_kernel, layer=layer, n_pages=n_pages, t_new=t_new),
        grid_spec=grid_spec,
        out_shape=jax.ShapeDtypeStruct((n, MLA_DIM), out_dtype),
        compiler_params=_cparams(1),
        name="attn_sample",
    )(page_table, q, knew, cache_kv, cache_kr, wuv, g)


def _out_ffn_kernel(x_ref, pool_ref, ssd_ref, mla_ref, wout_ref, g_ref, wup_ref, wdn_ref, gf_ref, o_ref,
                    xn_sc, hn_sc, acc_sc, *, final):
    k = pl.program_id(1)

    @pl.when(k == 0)
    def _():
        y = (_dot(pool_ref[...].astype(BF), wout_ref[0:POOL_DIM])
             + _dot(ssd_ref[...].astype(BF), wout_ref[POOL_DIM:POOL_DIM + SSD_DIM])
             + _dot(mla_ref[...].astype(BF), wout_ref[POOL_DIM + SSD_DIM:]))
        xn = x_ref[...] + y
        xn_sc[...] = xn
        hn_sc[...] = _rms(xn, g_ref[...]).astype(BF)
        acc_sc[...] = jnp.zeros_like(acc_sc)

    a = jnp.maximum(_dot(hn_sc[...], wup_ref[...]), 0.0)
    acc_sc[...] += _dot((a * a).astype(BF), wdn_ref[...])

    @pl.when(k == pl.num_programs(1) - 1)
    def _():
        x_out = xn_sc[...] + acc_sc[...]
        o_ref[...] = _rms(x_out, gf_ref[...]) if final else x_out


def _out_ffn(x, pool, ssd, mla, wout, g, wup, wdn, gfinal, final):
    n = x.shape[0]
    tm = _row_tile(n, FFN_ROWS)
    tf = FFN_COLS
    rowblk = lambda i, k: (i, 0)
    par = lambda i, k: (0, 0)
    return pl.pallas_call(
        functools.partial(_out_ffn_kernel, final=final),
        grid=(n // tm, FFN_DIM // tf),
        in_specs=[pl.BlockSpec((tm, D_MODEL), rowblk),
                  pl.BlockSpec((tm, POOL_DIM), rowblk),
                  pl.BlockSpec((tm, SSD_DIM), rowblk),
                  pl.BlockSpec((tm, MLA_DIM), rowblk),
                  pl.BlockSpec((D_MODEL, D_MODEL), par),
                  pl.BlockSpec((1, D_MODEL), par),
                  pl.BlockSpec((D_MODEL, tf), lambda i, k: (0, k)),
                  pl.BlockSpec((tf, D_MODEL), lambda i, k: (k, 0)),
                  pl.BlockSpec((1, D_MODEL), par)],
        out_specs=pl.BlockSpec((tm, D_MODEL), rowblk),
        out_shape=jax.ShapeDtypeStruct((n, D_MODEL), F32),
        scratch_shapes=[pltpu.VMEM((tm, D_MODEL), F32), pltpu.VMEM((tm, D_MODEL), BF),
                        pltpu.VMEM((tm, D_MODEL), F32)],
        compiler_params=_cparams(2),
        name="out_ffn",
    )(x, pool, ssd, mla, wout, g, wup, wdn, gfinal)


def _pad_cols(a, width):
    return jnp.pad(a, [(0, 0)] * (a.ndim - 1) + [(0, width - a.shape[-1])])


def _rot_half_cols(w):
    half = MLA_ROPE // 2
    return jnp.concatenate([-w[..., half:], w[..., :half]], axis=-1)


def _prep_weights(w_in, pool_w, w_uq, w_uk, w_uv, w_out, w_up, w_down):
    depth = w_in.shape[0]
    parts, start = [], 0
    for wd in IN_WIDTHS:
        parts.append(w_in[..., start:start + wd])
        start += wd
    u, z, xbc, dt, cq, ckv, kpe = parts
    w_in_p = jnp.concatenate([u, z, xbc, cq, ckv, _pad_cols(kpe, LANES), _pad_cols(_rot_half_cols(kpe), LANES),
                              _pad_cols(dt, LANES)], axis=-1).astype(BF)
    eye4 = jnp.eye(len(POOL_WINDOWS), dtype=F32)
    pool_bd = (pool_w[:, :, :, None, :] * eye4[None, :, None, :, None]).reshape(depth, POOL_DIM, POOL_DIM).astype(BF)
    wq = w_uq.reshape(depth, MLA_Q_LORA, MLA_HEADS, MLA_NOPE + MLA_ROPE)
    wq_nope = wq[..., :MLA_NOPE].reshape(depth, MLA_Q_LORA, Q_NOPE_W)
    wq_pe = wq[..., MLA_NOPE:]
    wq_p = jnp.concatenate([wq_nope,
                            _pad_cols(wq_pe, LANES).reshape(depth, MLA_Q_LORA, Q_PE_W),
                            _pad_cols(_rot_half_cols(wq_pe), LANES).reshape(depth, MLA_Q_LORA, Q_PE_W)],
                           axis=-1).astype(BF)
    eye6 = jnp.eye(MLA_HEADS, dtype=F32)
    uk = jnp.transpose(w_uk, (0, 2, 3, 1))
    wuk_bd = (uk[:, :, :, None, :] * eye6[None, :, None, :, None]).reshape(
        depth, Q_NOPE_W, MLA_HEADS * MLA_KV_LORA).astype(BF)
    uv = jnp.transpose(w_uv, (0, 2, 1, 3))
    wuv_pad = (uv[:, :, :, None, :] * eye6[None, :, None, :, None]).reshape(
        depth, MLA_HEADS, MLA_KV_LORA, MLA_DIM).astype(BF)
    return w_in_p, pool_bd, wq_p, wuk_bd, wuv_pad, w_out.astype(BF), w_up.astype(BF), w_down.astype(BF)


def _rope_tables(pos, reps):
    half = MLA_ROPE // 2
    inv = ROPE_THETA ** (-jnp.arange(half, dtype=F32) * (2.0 / MLA_ROPE))
    ang = pos.astype(F32)[:, None] * inv[None, :]
    cos = jnp.cos(ang)
    sin = jnp.sin(ang)
    cos = _pad_cols(jnp.concatenate([cos, cos], axis=-1), LANES)
    sin = _pad_cols(jnp.concatenate([sin, sin], axis=-1), LANES)
    return jnp.tile(cos, (reps, 1)), jnp.tile(sin, (reps, 1))


def _state_to_rows(h):
    return h.reshape(h.shape[0], SSD_DIM, SSD_STATE)


def _group_layer(x, batch, lw, pos0, pool_hist, conv_hist, ssm_h0, tables, attend, bb, act_dtype):
    n = x.shape[0]
    l = n // batch
    u, z, xbc, cq, ckv, kpe, kper, dt = _in_proj(x, lw["norm_mix_g"], lw["w_in"])

    pool_out, pool_tail = _pool(u.reshape(batch, l, POOL_DIM), pool_hist, lw["pool_bd"], lw["pool_scale"],
                                pos0, act_dtype)
    ssd_out, conv_tail, hbig = _ssd(z.reshape(batch, l, SSD_DIM), xbc.reshape(batch, l, SSD_CONV_DIM),
                                    dt.reshape(batch, l, LANES), conv_hist, ssm_h0,
                                    lw["conv_w"], lw["conv_b"], lw["dt_bias"], lw["a_log"], lw["d_skip"],
                                    lw["ssd_norm_g"], bb, act_dtype)
    q, kcat, ckv_n, kpe_r = _mla_prep(cq, ckv, kpe, kper, tables[0], tables[1], lw["q_norm_g"], lw["kv_norm_g"],
                                      lw["wq"], lw["wuk_bd"], act_dtype)
    mla_out = attend(q, kcat)
    x_new = _out_ffn(x, pool_out.reshape(n, POOL_DIM), ssd_out.reshape(n, SSD_DIM), mla_out,
                     lw["w_out"], lw["norm_ffn_g"], lw["w_up"], lw["w_down"], lw["final_g"], lw["final"])
    new_pool = pool_tail[:, HIST_ROWS - POOL_HIST:]
    new_conv = conv_tail[:, CONV_ROWS - (SSD_CONV - 1):]
    new_ssm = hbig.reshape(batch, SSD_HEADS, SSD_HEADDIM, SSD_STATE)
    return (x_new, ckv_n.reshape(batch, l, MLA_KV_LORA), kpe_r.reshape(batch, l, MLA_ROPE),
            new_pool, new_conv, new_ssm)


def kernel(x_prompt, x_sample, cache_kv_latent, cache_k_rope, state_pool, state_conv, state_ssm, page_table,
           norm_mix_g, w_in, pool_w, pool_scale, conv_w, conv_b, dt_bias, a_log, d_skip, ssd_norm_g, q_norm_g,
           w_uq, kv_norm_g, w_uk, w_uv, mla_out_g, w_out, norm_ffn_g, w_up, w_down, final_norm_g):
    depth = w_in.shape[0]
    bp, seq, _ = x_prompt.shape
    db, dec_seq, _ = x_sample.shape
    past_len = page_table.shape[1] * PAGE_SIZE

    w_in_p, pool_bd, wq_p, wuk_bd, wuv_pad, w_out_b, w_up_b, w_down_b = _prep_weights(
        w_in, pool_w, w_uq, w_uk, w_uv, w_out, w_up, w_down)
    tables_p = _rope_tables(jnp.arange(seq, dtype=jnp.int32), 1)
    tile_s = _row_tile(db * dec_seq, 512)
    tables_s = _rope_tables(past_len + jnp.arange(dec_seq, dtype=jnp.int32), tile_s // dec_seq)

    cache_kr_t = jnp.swapaxes(cache_k_rope, 2, 3)

    xp = x_prompt.reshape(bp * seq, D_MODEL)
    xs = x_sample.reshape(db * dec_seq, D_MODEL)
    zero_pool = jnp.zeros((bp, HIST_ROWS, POOL_DIM), F32)
    zero_conv = jnp.zeros((bp, CONV_ROWS, SSD_CONV_DIM), F32)
    zero_ssm = jnp.zeros((bp, SSD_DIM, SSD_STATE), F32)
    outs_p, outs_s = [], []
    for l in range(depth):
        lw = {
            "norm_mix_g": norm_mix_g[l][None], "w_in": w_in_p[l], "pool_bd": pool_bd[l],
            "pool_scale": pool_scale[l][None], "conv_w": conv_w[l], "conv_b": conv_b[l][None],
            "dt_bias": _pad_cols(dt_bias[l][None], LANES), "a_log": _pad_cols(a_log[l][None], LANES),
            "d_skip": jnp.repeat(d_skip[l], SSD_HEADDIM)[None], "ssd_norm_g": ssd_norm_g[l][None],
            "q_norm_g": q_norm_g[l][None], "kv_norm_g": kv_norm_g[l][None], "wq": wq_p[l], "wuk_bd": wuk_bd[l],
            "w_out": w_out_b[l], "norm_ffn_g": norm_ffn_g[l][None], "w_up": w_up_b[l], "w_down": w_down_b[l],
            "final_g": final_norm_g[None], "final": l == depth - 1,
        }
        wuv_l = wuv_pad[l]
        og = mla_out_g[l][None]

        attend_p = lambda q, k: _attn_prompt(q, k, wuv_l, og, bp, BF)
        res = _group_layer(xp, bp, lw, 0, zero_pool, zero_conv, zero_ssm, tables_p, attend_p, min(bp, SSD_SEQ_UNROLL), BF)
        xp = res[0]
        outs_p.append(res[1:])

        attend_s = lambda q, k: _attn_sample(page_table, q, k, cache_kv_latent, cache_kr_t, wuv_l, og, l, F32)
        pool_hist = jnp.pad(state_pool[l], ((0, 0), (HIST_ROWS - POOL_HIST, 0), (0, 0)))
        conv_hist = jnp.pad(state_conv[l], ((0, 0), (CONV_ROWS - (SSD_CONV - 1), 0), (0, 0)))
        res = _group_layer(xs, db, lw, past_len, pool_hist, conv_hist, _state_to_rows(state_ssm[l]),
                           tables_s, attend_s, 8, F32)
        xs = res[0]
        outs_s.append(res[1:])

    y_prompt = xp.reshape(bp, seq, D_MODEL)
    y_sample = xs.reshape(db, dec_seq, D_MODEL)
    stack = lambda outs, k: jnp.stack([o[k] for o in outs])
    return (y_prompt, y_sample,
            stack(outs_p, 0), stack(outs_p, 1), stack(outs_p, 2), stack(outs_p, 3), stack(outs_p, 4),
            stack(outs_s, 0), stack(outs_s, 1), stack(outs_s, 2), stack(outs_s, 3), stack(outs_s, 4))
```

```python
import functools

import jax
import jax.numpy as jnp
from jax import lax
from jax.experimental import pallas as pl
from jax.experimental.pallas import tpu as pltpu

D_MODEL = 1024
POOL_WINDOWS = (2, 4, 8, 16)
POOL_DIM = 256
POOL_GROUP = 64
POOL_HIST = 15
SSD_HEADDIM = 64
SSD_HEADS = 6
SSD_DIM = 384
SSD_GROUPS = 2
SSD_STATE = 64
SSD_CONV = 4
SSD_CONV_DIM = 640
SSD_CHUNK = 128
MLA_HEADS = 6
MLA_NOPE = 64
MLA_ROPE = 32
MLA_V = 64
MLA_Q_LORA = 256
MLA_KV_LORA = 128
MLA_DIM = 384
MLA_SCALE = (MLA_NOPE + MLA_ROPE) ** -0.5
ROPE_THETA = 10000.0
FFN_DIM = 4096
NORM_EPS = 1e-6
PAGE_SIZE = 128
IN_WIDTHS = (POOL_DIM, SSD_DIM, SSD_CONV_DIM, SSD_HEADS, MLA_Q_LORA, MLA_KV_LORA, MLA_ROPE)

LANES = 128
SUBLANES = 8
HIST_ROWS = 16
CONV_ROWS = 8
QK_WIDTH = 2 * LANES
VMEM_LIMIT = 48 * 1024 * 1024
ATTN_BQ = 512
ATTN_BK = 512
ATTN_HEAD_UNROLL = 6
LOG2E = 1.4426950408889634
FFN_ROWS = 1024
FFN_COLS = 1024
SSD_SEQ_UNROLL = 1

BF = jnp.bfloat16
F32 = jnp.float32


def _cparams(n_axes):
    return pltpu.CompilerParams(dimension_semantics=("arbitrary",) * n_axes,
                                vmem_limit_bytes=VMEM_LIMIT)


def _rms(x, g):
    return x * lax.rsqrt(jnp.mean(x * x, axis=-1, keepdims=True) + NORM_EPS) * g


def _dot(a, b):
    return jnp.dot(a, b, preferred_element_type=F32)


def _dot_nt(a, b):
    return lax.dot_general(a, b, (((1,), (1,)), ((), ())), preferred_element_type=F32)


def _row_tile(n, target):
    t = min(n, target)
    while n % t:
        t //= 2
    return t


IN_OUT_WIDTHS = (POOL_DIM, SSD_DIM, SSD_CONV_DIM, MLA_Q_LORA, MLA_KV_LORA, LANES, LANES, LANES)


def _in_proj_kernel(x_ref, g_ref, w_ref, hist_ref, pw_ref, pscale_ref, pool_ref, tail_ref, *rest,
                    pos0, seg, nseg, tiles_per_seq):
    out_refs, (u_sc, carry_sc) = rest[:-2], rest[-2:]
    i = pl.program_id(0)
    h = _rms(x_ref[...], g_ref[...]).astype(BF)
    proj = _dot(h, w_ref[...])
    u_sc[...] = proj[:, :POOL_DIM]
    off = POOL_DIM
    for o_ref in out_refs:
        width = o_ref.shape[-1]
        o_ref[...] = proj[:, off:off + width]
        off += width

    t = lax.rem(i, tiles_per_seq)
    lane = lax.broadcasted_iota(jnp.int32, (1, POOL_DIM), 1)
    g0, g1, g2 = lane < POOL_GROUP, lane < 2 * POOL_GROUP, lane < 3 * POOL_GROUP
    width = jnp.where(g0, 2, jnp.where(g1, 4, jnp.where(g2, 8, 16)))
    pos = pos0 + t * seg + lax.broadcasted_iota(jnp.int32, (seg, 1), 0)
    cnt = jnp.minimum(width, pos + 1).astype(F32)

    def per_segment(j, carry):
        @pl.when(t == 0)
        def _():
            carry_sc[j] = hist_ref[j]

        rows = pl.ds(pl.multiple_of(j * seg, seg), seg)
        u = u_sc[rows, :]
        ext = jnp.concatenate([carry_sc[j], u], axis=0)
        s2 = ext + pltpu.roll(ext, 1, 0)
        s4 = s2 + pltpu.roll(s2, 2, 0)
        s8 = s4 + pltpu.roll(s4, 4, 0)
        s16 = s8 + pltpu.roll(s8, 8, 0)
        wsum = jnp.where(g0, s2, jnp.where(g1, s4, jnp.where(g2, s8, s16)))[HIST_ROWS:]
        m = wsum / cnt - u
        out = _dot(m.astype(BF), pw_ref[...]) * pscale_ref[...]
        pool_ref[rows, :] = out.astype(pool_ref.dtype)
        last = ext[seg:seg + HIST_ROWS]
        carry_sc[j] = last
        tail_ref[j] = last
        return carry

    if nseg == 1:
        per_segment(0, 0)
    else:
        lax.fori_loop(0, nseg, per_segment, 0)


def _in_proj(x, g, w, hist, pool_w, pool_scale, seq_len, pos0, pool_dtype):
    n = x.shape[0]
    tm = _row_tile(n, 512)
    seg = min(seq_len, tm)
    nseg = tm // seg
    tiles_per_seq = seq_len // seg
    wtot = w.shape[1]
    per_tile = lambda i: (i, 0)
    fixed = lambda i: (0, 0)
    seqs = lambda i: (i // tiles_per_seq, 0, 0)
    widths = IN_OUT_WIDTHS[1:]
    return pl.pallas_call(
        functools.partial(_in_proj_kernel, pos0=pos0, seg=seg, nseg=nseg, tiles_per_seq=tiles_per_seq),
        grid=(n // tm,),
        in_specs=[pl.BlockSpec((tm, D_MODEL), per_tile),
                  pl.BlockSpec((1, D_MODEL), fixed),
                  pl.BlockSpec((D_MODEL, wtot), fixed),
                  pl.BlockSpec((nseg, HIST_ROWS, POOL_DIM), seqs),
                  pl.BlockSpec((POOL_DIM, POOL_DIM), fixed),
                  pl.BlockSpec((1, POOL_DIM), fixed)],
        out_specs=[pl.BlockSpec((tm, POOL_DIM), per_tile),
                   pl.BlockSpec((nseg, HIST_ROWS, POOL_DIM), seqs)]
                  + [pl.BlockSpec((tm, wd), per_tile) for wd in widths],
        out_shape=[jax.ShapeDtypeStruct((n, POOL_DIM), pool_dtype),
                   jax.ShapeDtypeStruct((n // seq_len, HIST_ROWS, POOL_DIM), F32)]
                  + [jax.ShapeDtypeStruct((n, wd), F32) for wd in widths],
        scratch_shapes=[pltpu.VMEM((tm, POOL_DIM), F32), pltpu.VMEM((nseg, HIST_ROWS, POOL_DIM), F32)],
        compiler_params=_cparams(1),
        name="in_proj",
    )(x, g, w, hist, pool_w, pool_scale)


def _expand_heads(x):
    q = x.shape[0]
    col = lax.broadcasted_iota(jnp.int32, (1, SSD_DIM), 1)
    out = jnp.zeros((q, SSD_DIM), F32)
    for h in range(SSD_HEADS):
        sel = (col >= h * SSD_HEADDIM) & (col < (h + 1) * SSD_HEADDIM)
        out = jnp.where(sel, jnp.broadcast_to(x[:, h:h + 1], (q, SSD_DIM)), out)
    return out


def _ssd_kernel(z_ref, xbc_ref, dt_ref, hist_ref, h0_ref, convw_ref, convb_ref, dtb_ref, alog_ref,
                dskip_ref, ng_ref, y_ref, newconv_ref, ht_ref, ext_sc, hbig_sc, *, rows, bb):
    c = pl.program_id(1)
    q = SSD_CHUNK
    pad = q - rows

    def padded(x):
        if pad == 0:
            return x
        return jnp.concatenate([x, jnp.zeros((pad, x.shape[1]), x.dtype)], axis=0)

    row = lax.broadcasted_iota(jnp.int32, (q, 1), 0)
    col_t = lax.broadcasted_iota(jnp.int32, (1, q), 1)
    causal = row >= col_t
    lane = lax.broadcasted_iota(jnp.int32, (1, LANES), 1)
    row384 = lax.broadcasted_iota(jnp.int32, (SSD_DIM, 1), 0)
    upper_rows = row384 >= SSD_DIM // SSD_GROUPS
    state_valid = jnp.where(upper_rows, 1, 0) == jnp.where(lane >= SSD_STATE, 1, 0)
    a_neg = -jnp.exp(alog_ref[...])

    def per_seq(i, carry):
        @pl.when(c == 0)
        def _():
            ext_sc[i, 0:CONV_ROWS, :] = hist_ref[i]
            hbig_sc[i] = jnp.zeros((SSD_DIM, LANES), F32)
            hbig_sc[i, :, 0:SSD_STATE] = h0_ref[i]
            h0 = hbig_sc[i]
            hbig_sc[i] = jnp.where(upper_rows, pltpu.roll(h0, SSD_STATE, 1), h0)

        ext_sc[i, CONV_ROWS:CONV_ROWS + q, :] = padded(xbc_ref[i])
        ext = ext_sc[i]
        conv = convb_ref[...] + convw_ref[SSD_CONV - 1:SSD_CONV, :] * ext[CONV_ROWS:]
        for back in range(1, SSD_CONV):
            k = SSD_CONV - 1 - back
            conv = conv + convw_ref[k:k + 1, :] * pltpu.roll(ext, back, 0)[CONV_ROWS:]
        tail = ext_sc[i, rows:rows + CONV_ROWS, :]
        newconv_ref[i] = tail
        ext_sc[i, 0:CONV_ROWS, :] = tail

        act = conv * jax.nn.sigmoid(conv)
        xs = act[:, :SSD_DIM]
        bmat = act[:, SSD_DIM:SSD_DIM + LANES]
        cmat = act[:, SSD_DIM + LANES:]

        dtr = padded(dt_ref[i]) + dtb_ref[...]
        dt = jnp.maximum(dtr, 0.0) + jnp.log(1.0 + jnp.exp(-jnp.abs(dtr)))
        dt = jnp.where(row < rows, dt, 0.0)
        a_cs = dt * a_neg
        k = 1
        while k < q:
            a_cs = a_cs + jnp.where(row >= k, pltpu.roll(a_cs, k, 0), 0.0)
            k *= 2
        a_cs_t = a_cs.T
        a_last = a_cs[q - 1:q, :]
        decay_to_end = jnp.exp(a_last - a_cs)
        decay_from_start = jnp.exp(a_cs)
        chunk_decay = jnp.exp(a_last)

        dt_t = dt.T
        xdec = xs * _expand_heads(dt * decay_to_end)

        bmat_b = bmat.astype(BF)
        cb = [_dot_nt(jnp.where((lane >= g * SSD_STATE) & (lane < (g + 1) * SSD_STATE), cmat, 0.0).astype(BF),
                      bmat_b) for g in range(SSD_GROUPS)]
        tiles = []
        for t in range(SSD_DIM // LANES):
            xt = xs[:, t * LANES:(t + 1) * LANES]
            yt = jnp.zeros((q, LANES), F32)
            for h in (2 * t, 2 * t + 1):
                seg = a_cs[:, h:h + 1] - a_cs_t[h:h + 1, :]
                lmat = jnp.exp(jnp.where(causal, seg, -jnp.inf))
                m = cb[h // (SSD_HEADS // SSD_GROUPS)] * lmat * dt_t[h:h + 1, :]
                own = (lane < SSD_HEADDIM) if h % 2 == 0 else (lane >= SSD_HEADDIM)
                yt = yt + _dot(m.astype(BF), jnp.where(own, xt, 0.0).astype(BF))
            tiles.append(yt)
        y = jnp.concatenate(tiles, axis=1)

        hbig = hbig_sc[i]
        y_off = _dot_nt(cmat.astype(BF), hbig.astype(BF)) * _expand_heads(decay_from_start)
        y = y + y_off + dskip_ref[...] * xs
        zf = padded(z_ref[i])
        gated = y * (zf * jax.nn.sigmoid(zf))
        y_ref[i] = _rms(gated, ng_ref[...])[:rows].astype(y_ref.dtype)

        new_states = _dot(xdec.T.astype(BF), bmat_b)
        cd_rows = _expand_heads(jnp.broadcast_to(chunk_decay, (q, LANES))).T[:, :LANES]
        hnew = hbig * cd_rows + jnp.where(state_valid, new_states, 0.0)
        hbig_sc[i] = hnew
        ht_ref[i] = jnp.where(upper_rows, pltpu.roll(hnew, SSD_STATE, 1), hnew)[:, :SSD_STATE]
        return carry

    if bb <= SSD_SEQ_UNROLL:
        for i in range(bb):
            per_seq(i, 0)
    else:
        lax.fori_loop(0, bb, per_seq, 0)


def _ssd(z, xbc, dt, hist, hbig0, convw, convb, dtb, alog, dskip, ng, bb, out_dtype):
    b, l, _ = z.shape
    rows = min(l, SSD_CHUNK)
    nc = l // rows
    seq = lambda i, c: (i, c, 0)
    fixed3 = lambda i, c: (i, 0, 0)
    par = lambda i, c: (0, 0)
    return pl.pallas_call(
        functools.partial(_ssd_kernel, rows=rows, bb=bb),
        grid=(b // bb, nc),
        in_specs=[pl.BlockSpec((bb, rows, SSD_DIM), seq),
                  pl.BlockSpec((bb, rows, SSD_CONV_DIM), seq),
                  pl.BlockSpec((bb, rows, LANES), seq),
                  pl.BlockSpec((bb, CONV_ROWS, SSD_CONV_DIM), fixed3),
                  pl.BlockSpec((bb, SSD_DIM, SSD_STATE), fixed3),
                  pl.BlockSpec((SSD_CONV, SSD_CONV_DIM), par),
                  pl.BlockSpec((1, SSD_CONV_DIM), par),
                  pl.BlockSpec((1, LANES), par),
                  pl.BlockSpec((1, LANES), par),
                  pl.BlockSpec((1, SSD_DIM), par),
                  pl.BlockSpec((1, SSD_DIM), par)],
        out_specs=[pl.BlockSpec((bb, rows, SSD_DIM), seq),
                   pl.BlockSpec((bb, CONV_ROWS, SSD_CONV_DIM), fixed3),
                   pl.BlockSpec((bb, SSD_DIM, SSD_STATE), fixed3)],
        out_shape=[jax.ShapeDtypeStruct((b, l, SSD_DIM), out_dtype),
                   jax.ShapeDtypeStruct((b, CONV_ROWS, SSD_CONV_DIM), F32),
                   jax.ShapeDtypeStruct((b, SSD_DIM, SSD_STATE), F32)],
        scratch_shapes=[pltpu.VMEM((bb, CONV_ROWS + SSD_CHUNK, SSD_CONV_DIM), F32),
                        pltpu.VMEM((bb, SSD_DIM, LANES), F32)],
        compiler_params=_cparams(2),
        name="ssd_mix",
    )(z, xbc, dt, hist, hbig0, convw, convb, dtb, alog, dskip, ng)


Q_NOPE_W = MLA_HEADS * MLA_NOPE
Q_PE_W = MLA_HEADS * LANES


def _mla_prep_kernel(cq_ref, ckv_ref, kpe_ref, kper_ref, cos_ref, sin_ref, qg_ref, kg_ref, wq_ref, wuk_ref,
                     q_ref, k_ref, ckvn_ref, kr_ref):
    hq = _rms(cq_ref[...], qg_ref[...]).astype(BF)
    q_nope = _dot(hq, wq_ref[:, :Q_NOPE_W])
    q_pe = _dot(hq, wq_ref[:, Q_NOPE_W:Q_NOPE_W + Q_PE_W])
    q_pe_rot = _dot(hq, wq_ref[:, Q_NOPE_W + Q_PE_W:])
    q_lat = _dot(q_nope.astype(BF), wuk_ref[...])
    cos = cos_ref[...]
    sin = sin_ref[...]
    for h in range(MLA_HEADS):
        sl = slice(h * LANES, (h + 1) * LANES)
        q_ref[h, :, 0:LANES] = q_lat[:, sl].astype(q_ref.dtype)
        q_ref[h, :, LANES:QK_WIDTH] = (q_pe[:, sl] * cos + q_pe_rot[:, sl] * sin).astype(q_ref.dtype)
    ckvn = _rms(ckv_ref[...], kg_ref[...])
    ckvn_ref[...] = ckvn
    kr = kpe_ref[...] * cos + kper_ref[...] * sin
    kr_ref[...] = kr[:, :MLA_ROPE]
    k_ref[:, 0:LANES] = ckvn.astype(k_ref.dtype)
    lane = lax.broadcasted_iota(jnp.int32, (1, LANES), 1)
    k_ref[:, LANES:QK_WIDTH] = jnp.where(lane == LANES - 1, 1.0, kr).astype(k_ref.dtype)


def _mla_prep(cq, ckv, kpe, kper, cos, sin, qg, kg, wq, wuk, qk_dtype):
    n = cq.shape[0]
    tm = _row_tile(n, 512)
    ntab = cos.shape[0] // tm
    rowblk = lambda i: (i, 0)
    tab = lambda i: (i % ntab, 0)
    par = lambda i: (0, 0)
    return pl.pallas_call(
        _mla_prep_kernel,
        grid=(n // tm,),
        in_specs=[pl.BlockSpec((tm, MLA_Q_LORA), rowblk),
                  pl.BlockSpec((tm, MLA_KV_LORA), rowblk),
                  pl.BlockSpec((tm, LANES), rowblk),
                  pl.BlockSpec((tm, LANES), rowblk),
                  pl.BlockSpec((tm, LANES), tab),
                  pl.BlockSpec((tm, LANES), tab),
                  pl.BlockSpec((1, MLA_Q_LORA), par),
                  pl.BlockSpec((1, MLA_KV_LORA), par),
                  pl.BlockSpec(wq.shape, par),
                  pl.BlockSpec(wuk.shape, par)],
        out_specs=[pl.BlockSpec((MLA_HEADS, tm, QK_WIDTH), lambda i: (0, i, 0)),
                   pl.BlockSpec((tm, QK_WIDTH), rowblk),
                   pl.BlockSpec((tm, MLA_KV_LORA), rowblk),
                   pl.BlockSpec((tm, MLA_ROPE), rowblk)],
        out_shape=[jax.ShapeDtypeStruct((MLA_HEADS, n, QK_WIDTH), qk_dtype),
                   jax.ShapeDtypeStruct((n, QK_WIDTH), qk_dtype),
                   jax.ShapeDtypeStruct((n, MLA_KV_LORA), F32),
                   jax.ShapeDtypeStruct((n, MLA_ROPE), F32)],
        compiler_params=_cparams(1),
        name="mla_prep",
    )(cq, ckv, kpe, kper, cos, sin, qg, kg, wq, wuk)


def _value_up_norm(o_lat, rows, wuv_ref, g_ref):
    out = jnp.zeros((rows, MLA_DIM), F32)
    for h in range(MLA_HEADS):
        out = out + _dot(o_lat[h * rows:(h + 1) * rows].astype(BF), wuv_ref[h])
    return _rms(out, g_ref[...])


def _attn_prompt_kernel(q_ref, k_ref, wuv_ref, g_ref, o_ref, m_sc, acc_sc, *, bq, bk, hu):
    i = pl.program_id(1)
    m_sc[...] = jnp.full(m_sc.shape, -jnp.inf, F32)
    acc_sc[...] = jnp.zeros(acc_sc.shape, F32)
    exp2_scale = MLA_SCALE * LOG2E

    def head_step(h, kb, mask):
        s = _dot_nt(q_ref[h], kb)
        if mask is not None:
            s = jnp.where(mask, s, -jnp.inf)
        m_old = m_sc[h]
        m_new = jnp.maximum(m_old, jnp.max(s, axis=1, keepdims=True))
        alpha = jnp.exp2((m_old - m_new) * exp2_scale)
        p = jnp.exp2((s - jnp.concatenate([m_new] * (s.shape[1] // LANES), axis=1)) * exp2_scale)
        acc_sc[h] = jnp.concatenate([alpha, alpha], axis=1) * acc_sc[h] + _dot(p.astype(BF), kb)
        m_sc[h] = m_new

    def kv_step(j, width, masked):
        start = pl.multiple_of(j * bk, bk)
        mask = None
        if masked:
            t_q = i * bq + lax.broadcasted_iota(jnp.int32, (bq, width), 0)
            t_k = start + lax.broadcasted_iota(jnp.int32, (bq, width), 1)
            mask = t_k <= t_q

        def head_group(gi, carry):
            kb = k_ref[pl.ds(start, width), :]
            for u in range(hu):
                head_step(gi * hu + u, kb, mask)
            return carry

        lax.fori_loop(0, MLA_HEADS // hu, head_group, 0)

    n_full = (i * bq) // bk

    def body(j, carry):
        kv_step(j, bk, False)
        return carry

    lax.fori_loop(0, n_full, body, 0)
    rest = (i + 1) * bq - n_full * bk
    for width in range(bq, bk + 1, bq):
        @pl.when(rest == width)
        def _():
            kv_step(n_full, width, True)
    out = jnp.zeros((bq, MLA_DIM), F32)
    for h in range(MLA_HEADS):
        acc = acc_sc[h]
        o_lat = acc[:, :MLA_KV_LORA] / acc[:, QK_WIDTH - 1:QK_WIDTH]
        out = out + _dot(o_lat.astype(BF), wuv_ref[h])
    o_ref[...] = _rms(out, g_ref[...]).astype(o_ref.dtype)


def _attn_prompt(q, k, wuv, g, batch, out_dtype):
    n = k.shape[0]
    l = n // batch
    bq = _row_tile(l, ATTN_BQ)
    bk = _row_tile(l, ATTN_BK)
    nq = l // bq
    return pl.pallas_call(
        functools.partial(_attn_prompt_kernel, bq=bq, bk=bk, hu=ATTN_HEAD_UNROLL),
        grid=(batch, nq),
        in_specs=[pl.BlockSpec((MLA_HEADS, bq, QK_WIDTH), lambda b, i: (0, b * nq + i, 0)),
                  pl.BlockSpec((l, QK_WIDTH), lambda b, i: (b, 0)),
                  pl.BlockSpec(wuv.shape, lambda b, i: (0, 0, 0)),
                  pl.BlockSpec((1, MLA_DIM), lambda b, i: (0, 0))],
        out_specs=pl.BlockSpec((bq, MLA_DIM), lambda b, i: (b * nq + i, 0)),
        out_shape=jax.ShapeDtypeStruct((n, MLA_DIM), out_dtype),
        scratch_shapes=[pltpu.VMEM((MLA_HEADS, bq, LANES), F32), pltpu.VMEM((MLA_HEADS, bq, QK_WIDTH), F32)],
        compiler_params=_cparams(2),
        name="attn_prompt",
    )(q, k, wuv, g)


def _attn_sample_kernel(pt_ref, q_ref, knew_ref, ckv_hbm, kr_hbm, wuv_ref, g_ref, o_ref,
                        kbuf, rbuf, sems, *, layer, n_pages, t_new):
    b = pl.program_id(0)
    nb = pl.num_programs(0)
    past = n_pages * PAGE_SIZE

    def start_fetch(seq, slot):
        for j in range(n_pages):
            pg = pt_ref[seq, j]
            rows = pl.ds(j * PAGE_SIZE, PAGE_SIZE)
            pltpu.make_async_copy(ckv_hbm.at[layer, pg], kbuf.at[slot, rows, :], sems.at[0, slot]).start()
            pltpu.make_async_copy(kr_hbm.at[layer, pg], rbuf.at[slot, :, rows], sems.at[1, slot]).start()

    def wait_fetch(slot):
        pltpu.make_async_copy(kbuf.at[1 - slot], kbuf.at[slot], sems.at[0, slot]).wait()
        pltpu.make_async_copy(rbuf.at[1 - slot], rbuf.at[slot], sems.at[1, slot]).wait()

    @pl.when(b == 0)
    def _():
        start_fetch(0, 0)

    @pl.when(b + 1 < nb)
    def _():
        start_fetch(b + 1, (b + 1) % 2)

    slot = b % 2
    wait_fetch(slot)

    rows = MLA_HEADS * t_new
    exp2_scale = MLA_SCALE * LOG2E
    q = q_ref[...].reshape(rows, QK_WIDTH).astype(BF)
    q_lat = q[:, :MLA_KV_LORA]
    q_pe = q[:, MLA_KV_LORA:MLA_KV_LORA + MLA_ROPE]

    kn = jnp.concatenate([knew_ref[...], jnp.zeros((LANES - t_new, QK_WIDTH), knew_ref.dtype)],
                         axis=0).astype(BF)
    sn = _dot_nt(q, kn)
    t_q = lax.broadcasted_iota(jnp.int32, (1, t_new, LANES), 1)
    t_k = lax.broadcasted_iota(jnp.int32, (1, t_new, LANES), 2)
    sn = jnp.where(t_k <= t_q, sn.reshape(MLA_HEADS, t_new, LANES), -jnp.inf).reshape(rows, LANES)
    kf = kbuf[slot].astype(BF)
    rf = rbuf[slot].astype(BF)
    s = _dot_nt(q_lat, kf) + _dot(q_pe, rf)
    m = jnp.maximum(jnp.max(s, axis=1, keepdims=True), jnp.max(sn, axis=1, keepdims=True))
    p = jnp.exp2((s - m) * exp2_scale)
    pn = jnp.exp2((sn - m) * exp2_scale)
    denom = jnp.sum(p, axis=1, keepdims=True) + jnp.sum(pn, axis=1, keepdims=True)
    o_lat = (_dot(p.astype(BF), kf) + _dot(pn.astype(BF), kn[:, :MLA_KV_LORA])) / denom
    o_ref[...] = _value_up_norm(o_lat, t_new, wuv_ref, g_ref).astype(o_ref.dtype)


def _attn_sample(page_table, q, knew, cache_kv, cache_kr, wuv, g, layer, out_dtype):
    nseq, n_pages = page_table.shape
    n = knew.shape[0]
    t_new = n // nseq
    past = n_pages * PAGE_SIZE
    grid_spec = pltpu.PrefetchScalarGridSpec(
        num_scalar_prefetch=1,
        grid=(nseq,),
        in_specs=[pl.BlockSpec((MLA_HEADS, t_new, QK_WIDTH), lambda b, pt: (0, b, 0)),
                  pl.BlockSpec((t_new, QK_WIDTH), lambda b, pt: (b, 0)),
                  pl.BlockSpec(memory_space=pl.ANY),
                  pl.BlockSpec(memory_space=pl.ANY),
                  pl.BlockSpec(wuv.shape, lambda b, pt: (0, 0, 0)),
                  pl.BlockSpec((1, MLA_DIM), lambda b, pt: (0, 0))],
        out_specs=pl.BlockSpec((t_new, MLA_DIM), lambda b, pt: (b, 0)),
        scratch_shapes=[pltpu.VMEM((2, past, MLA_KV_LORA), F32),
                        pltpu.VMEM((2, MLA_ROPE, past), F32),
                        pltpu.SemaphoreType.DMA((2, 2))],
    )
    return pl.pallas_call(
        functools.partial(_attn_sample_kernel, layer=layer, n_pages=n_pages, t_new=t_new),
        grid_spec=grid_spec,
        out_shape=jax.ShapeDtypeStruct((n, MLA_DIM), out_dtype),
        compiler_params=_cparams(1),
        name="attn_sample",
    )(page_table, q, knew, cache_kv, cache_kr, wuv, g)


def _out_ffn_kernel(x_ref, pool_ref, ssd_ref, mla_ref, wout_ref, g_ref, wup_ref, wdn_ref, gf_ref, o_ref,
                    xn_sc, hn_sc, acc_sc, *, final):
    k = pl.program_id(1)

    @pl.when(k == 0)
    def _():
        y = (_dot(pool_ref[...].astype(BF), wout_ref[0:POOL_DIM])
             + _dot(ssd_ref[...].astype(BF), wout_ref[POOL_DIM:POOL_DIM + SSD_DIM])
             + _dot(mla_ref[...].astype(BF), wout_ref[POOL_DIM + SSD_DIM:]))
        xn = x_ref[...] + y
        xn_sc[...] = xn
        hn_sc[...] = _rms(xn, g_ref[...]).astype(BF)
        acc_sc[...] = jnp.zeros_like(acc_sc)

    a = jnp.maximum(_dot(hn_sc[...], wup_ref[...]), 0.0)
    acc_sc[...] += _dot((a * a).astype(BF), wdn_ref[...])

    @pl.when(k == pl.num_programs(1) - 1)
    def _():
        x_out = xn_sc[...] + acc_sc[...]
        o_ref[...] = _rms(x_out, gf_ref[...]) if final else x_out


def _out_ffn(x, pool, ssd, mla, wout, g, wup, wdn, gfinal, final):
    n = x.shape[0]
    tm = _row_tile(n, FFN_ROWS)
    tf = FFN_COLS
    rowblk = lambda i, k: (i, 0)
    par = lambda i, k: (0, 0)
    return pl.pallas_call(
        functools.partial(_out_ffn_kernel, final=final),
        grid=(n // tm, FFN_DIM // tf),
        in_specs=[pl.BlockSpec((tm, D_MODEL), rowblk),
                  pl.BlockSpec((tm, POOL_DIM), rowblk),
                  pl.BlockSpec((tm, SSD_DIM), rowblk),
                  pl.BlockSpec((tm, MLA_DIM), rowblk),
                  pl.BlockSpec((D_MODEL, D_MODEL), par),
                  pl.BlockSpec((1, D_MODEL), par),
                  pl.BlockSpec((D_MODEL, tf), lambda i, k: (0, k)),
                  pl.BlockSpec((tf, D_MODEL), lambda i, k: (k, 0)),
                  pl.BlockSpec((1, D_MODEL), par)],
        out_specs=pl.BlockSpec((tm, D_MODEL), rowblk),
        out_shape=jax.ShapeDtypeStruct((n, D_MODEL), F32),
        scratch_shapes=[pltpu.VMEM((tm, D_MODEL), F32), pltpu.VMEM((tm, D_MODEL), BF),
                        pltpu.VMEM((tm, D_MODEL), F32)],
        compiler_params=_cparams(2),
        name="out_ffn",
    )(x, pool, ssd, mla, wout, g, wup, wdn, gfinal)


def _pad_cols(a, width):
    return jnp.pad(a, [(0, 0)] * (a.ndim - 1) + [(0, width - a.shape[-1])])


def _rot_half_cols(w):
    half = MLA_ROPE // 2
    return jnp.concatenate([-w[..., half:], w[..., :half]], axis=-1)


def _prep_weights(w_in, pool_w, w_uq, w_uk, w_uv, w_out, w_up, w_down):
    depth = w_in.shape[0]
    parts, start = [], 0
    for wd in IN_WIDTHS:
        parts.append(w_in[..., start:start + wd])
        start += wd
    u, z, xbc, dt, cq, ckv, kpe = parts
    w_in_p = jnp.concatenate([u, z, xbc, cq, ckv, _pad_cols(kpe, LANES), _pad_cols(_rot_half_cols(kpe), LANES),
                              _pad_cols(dt, LANES)], axis=-1).astype(BF)
    eye4 = jnp.eye(len(POOL_WINDOWS), dtype=F32)
    pool_bd = (pool_w[:, :, :, None, :] * eye4[None, :, None, :, None]).reshape(depth, POOL_DIM, POOL_DIM).astype(BF)
    wq = w_uq.reshape(depth, MLA_Q_LORA, MLA_HEADS, MLA_NOPE + MLA_ROPE)
    wq_nope = wq[..., :MLA_NOPE].reshape(depth, MLA_Q_LORA, Q_NOPE_W)
    wq_pe = wq[..., MLA_NOPE:]
    wq_p = jnp.concatenate([wq_nope,
                            _pad_cols(wq_pe, LANES).reshape(depth, MLA_Q_LORA, Q_PE_W),
                            _pad_cols(_rot_half_cols(wq_pe), LANES).reshape(depth, MLA_Q_LORA, Q_PE_W)],
                           axis=-1).astype(BF)
    eye6 = jnp.eye(MLA_HEADS, dtype=F32)
    uk = jnp.transpose(w_uk, (0, 2, 3, 1))
    wuk_bd = (uk[:, :, :, None, :] * eye6[None, :, None, :, None]).reshape(
        depth, Q_NOPE_W, MLA_HEADS * MLA_KV_LORA).astype(BF)
    uv = jnp.transpose(w_uv, (0, 2, 1, 3))
    wuv_pad = (uv[:, :, :, None, :] * eye6[None, :, None, :, None]).reshape(
        depth, MLA_HEADS, MLA_KV_LORA, MLA_DIM).astype(BF)
    return w_in_p, pool_bd, wq_p, wuk_bd, wuv_pad, w_out.astype(BF), w_up.astype(BF), w_down.astype(BF)


def _rope_tables(pos, reps):
    half = MLA_ROPE // 2
    inv = ROPE_THETA ** (-jnp.arange(half, dtype=F32) * (2.0 / MLA_ROPE))
    ang = pos.astype(F32)[:, None] * inv[None, :]
    cos = jnp.cos(ang)
    sin = jnp.sin(ang)
    cos = _pad_cols(jnp.concatenate([cos, cos], axis=-1), LANES)
    sin = _pad_cols(jnp.concatenate([sin, sin], axis=-1), LANES)
    return jnp.tile(cos, (reps, 1)), jnp.tile(sin, (reps, 1))


def _state_to_rows(h):
    return h.reshape(h.shape[0], SSD_DIM, SSD_STATE)


def _group_layer(x, batch, lw, pos0, pool_hist, conv_hist, ssm_h0, tables, attend, bb, act_dtype):
    n = x.shape[0]
    l = n // batch
    pool_out, pool_tail, z, xbc, cq, ckv, kpe, kper, dt = _in_proj(
        x, lw["norm_mix_g"], lw["w_in"], pool_hist, lw["pool_bd"], lw["pool_scale"], l, pos0, act_dtype)
    ssd_out, conv_tail, hbig = _ssd(z.reshape(batch, l, SSD_DIM), xbc.reshape(batch, l, SSD_CONV_DIM),
                                    dt.reshape(batch, l, LANES), conv_hist, ssm_h0,
                                    lw["conv_w"], lw["conv_b"], lw["dt_bias"], lw["a_log"], lw["d_skip"],
                                    lw["ssd_norm_g"], bb, act_dtype)
    q, kcat, ckv_n, kpe_r = _mla_prep(cq, ckv, kpe, kper, tables[0], tables[1], lw["q_norm_g"], lw["kv_norm_g"],
                                      lw["wq"], lw["wuk_bd"], act_dtype)
    mla_out = attend(q, kcat)
    x_new = _out_ffn(x, pool_out, ssd_out.reshape(n, SSD_DIM), mla_out,
                     lw["w_out"], lw["norm_ffn_g"], lw["w_up"], lw["w_down"], lw["final_g"], lw["final"])
    new_pool = pool_tail[:, HIST_ROWS - POOL_HIST:]
    new_conv = conv_tail[:, CONV_ROWS - (SSD_CONV - 1):]
    new_ssm = hbig.reshape(batch, SSD_HEADS, SSD_HEADDIM, SSD_STATE)
    return (x_new, ckv_n.reshape(batch, l, MLA_KV_LORA), kpe_r.reshape(batch, l, MLA_ROPE),
            new_pool, new_conv, new_ssm)


def kernel(x_prompt, x_sample, cache_kv_latent, cache_k_rope, state_pool, state_conv, state_ssm, page_table,
           norm_mix_g, w_in, pool_w, pool_scale, conv_w, conv_b, dt_bias, a_log, d_skip, ssd_norm_g, q_norm_g,
           w_uq, kv_norm_g, w_uk, w_uv, mla_out_g, w_out, norm_ffn_g, w_up, w_down, final_norm_g):
    depth = w_in.shape[0]
    bp, seq, _ = x_prompt.shape
    db, dec_seq, _ = x_sample.shape
    past_len = page_table.shape[1] * PAGE_SIZE

    w_in_p, pool_bd, wq_p, wuk_bd, wuv_pad, w_out_b, w_up_b, w_down_b = _prep_weights(
        w_in, pool_w, w_uq, w_uk, w_uv, w_out, w_up, w_down)
    tables_p = _rope_tables(jnp.arange(seq, dtype=jnp.int32), 1)
    tile_s = _row_tile(db * dec_seq, 512)
    tables_s = _rope_tables(past_len + jnp.arange(dec_seq, dtype=jnp.int32), tile_s // dec_seq)

    cache_kr_t = jnp.swapaxes(cache_k_rope, 2, 3)

    xp = x_prompt.reshape(bp * seq, D_MODEL)
    xs = x_sample.reshape(db * dec_seq, D_MODEL)
    zero_pool = jnp.zeros((bp, HIST_ROWS, POOL_DIM), F32)
    zero_conv = jnp.zeros((bp, CONV_ROWS, SSD_CONV_DIM), F32)
    zero_ssm = jnp.zeros((bp, SSD_DIM, SSD_STATE), F32)
    outs_p, outs_s = [], []
    for l in range(depth):
        lw = {
            "norm_mix_g": norm_mix_g[l][None], "w_in": w_in_p[l], "pool_bd": pool_bd[l],
            "pool_scale": pool_scale[l][None], "conv_w": conv_w[l], "conv_b": conv_b[l][None],
            "dt_bias": _pad_cols(dt_bias[l][None], LANES), "a_log": _pad_cols(a_log[l][None], LANES),
            "d_skip": jnp.repeat(d_skip[l], SSD_HEADDIM)[None], "ssd_norm_g": ssd_norm_g[l][None],
            "q_norm_g": q_norm_g[l][None], "kv_norm_g": kv_norm_g[l][None], "wq": wq_p[l], "wuk_bd": wuk_bd[l],
            "w_out": w_out_b[l], "norm_ffn_g": norm_ffn_g[l][None], "w_up": w_up_b[l], "w_down": w_down_b[l],
            "final_g": final_norm_g[None], "final": l == depth - 1,
        }
        wuv_l = wuv_pad[l]
        og = mla_out_g[l][None]

        attend_p = lambda q, k: _attn_prompt(q, k, wuv_l, og, bp, BF)
        res = _group_layer(xp, bp, lw, 0, zero_pool, zero_conv, zero_ssm, tables_p, attend_p, min(bp, SSD_SEQ_UNROLL), BF)
        xp = res[0]
        outs_p.append(res[1:])

        attend_s = lambda q, k: _attn_sample(page_table, q, k, cache_kv_latent, cache_kr_t, wuv_l, og, l, F32)
        pool_hist = jnp.pad(state_pool[l], ((0, 0), (HIST_ROWS - POOL_HIST, 0), (0, 0)))
        conv_hist = jnp.pad(state_conv[l], ((0, 0), (CONV_ROWS - (SSD_CONV - 1), 0), (0, 0)))
        res = _group_layer(xs, db, lw, past_len, pool_hist, conv_hist, _state_to_rows(state_ssm[l]),
                           tables_s, attend_s, 8, F32)
        xs = res[0]
        outs_s.append(res[1:])

    y_prompt = xp.reshape(bp, seq, D_MODEL)
    y_sample = xs.reshape(db, dec_seq, D_MODEL)
    stack = lambda outs, k: jnp.stack([o[k] for o in outs])
    return (y_prompt, y_sample,
            stack(outs_p, 0), stack(outs_p, 1), stack(outs_p, 2), stack(outs_p, 3), stack(outs_p, 4),
            stack(outs_s, 0), stack(outs_s, 1), stack(outs_s, 2), stack(outs_s, 3), stack(outs_s, 4))
```

```python
import functools

import jax
import jax.numpy as jnp
from jax import lax
from jax.experimental import pallas as pl
from jax.experimental.pallas import tpu as pltpu

D_MODEL = 1024
POOL_WINDOWS = (2, 4, 8, 16)
POOL_DIM = 256
POOL_GROUP = 64
POOL_HIST = 15
SSD_HEADDIM = 64
SSD_HEADS = 6
SSD_DIM = 384
SSD_GROUPS = 2
SSD_STATE = 64
SSD_CONV = 4
SSD_CONV_DIM = 640
SSD_CHUNK = 128
MLA_HEADS = 6
MLA_NOPE = 64
MLA_ROPE = 32
MLA_V = 64
MLA_Q_LORA = 256
MLA_KV_LORA = 128
MLA_DIM = 384
MLA_SCALE = (MLA_NOPE + MLA_ROPE) ** -0.5
ROPE_THETA = 10000.0
FFN_DIM = 4096
NORM_EPS = 1e-6
PAGE_SIZE = 128
IN_WIDTHS = (POOL_DIM, SSD_DIM, SSD_CONV_DIM, SSD_HEADS, MLA_Q_LORA, MLA_KV_LORA, MLA_ROPE)

LANES = 128
SUBLANES = 8
HIST_ROWS = 16
CONV_ROWS = 8
QK_WIDTH = 2 * LANES
VMEM_LIMIT = 48 * 1024 * 1024
ATTN_BQ = 512
ATTN_BK = 512
ATTN_HEAD_UNROLL = 6
LOG2E = 1.4426950408889634
FFN_ROWS = 512
FFN_COLS = 2048
SSD_SEQ_UNROLL = 1

BF = jnp.bfloat16
F32 = jnp.float32


def _cparams(n_axes):
    return pltpu.CompilerParams(dimension_semantics=("arbitrary",) * n_axes,
                                vmem_limit_bytes=VMEM_LIMIT)


def _rms(x, g):
    return x * lax.rsqrt(jnp.mean(x * x, axis=-1, keepdims=True) + NORM_EPS) * g


def _dot(a, b):
    return jnp.dot(a, b, preferred_element_type=F32)


def _dot_nt(a, b):
    return lax.dot_general(a, b, (((1,), (1,)), ((), ())), preferred_element_type=F32)


def _row_tile(n, target):
    t = min(n, target)
    while n % t:
        t //= 2
    return t


IN_OUT_WIDTHS = (POOL_DIM, SSD_DIM, SSD_CONV_DIM, MLA_Q_LORA, MLA_KV_LORA, LANES, LANES, LANES)


def _in_proj_kernel(x_ref, g_ref, w_ref, hist_ref, pw_ref, pscale_ref, pool_ref, tail_ref, *rest,
                    pos0, seg, nseg, tiles_per_seq):
    out_refs, (u_sc, carry_sc) = rest[:-2], rest[-2:]
    i = pl.program_id(0)
    h = _rms(x_ref[...], g_ref[...]).astype(BF)
    proj = _dot(h, w_ref[...])
    u_sc[...] = proj[:, :POOL_DIM]
    off = POOL_DIM
    for o_ref in out_refs:
        width = o_ref.shape[-1]
        o_ref[...] = proj[:, off:off + width]
        off += width

    t = lax.rem(i, tiles_per_seq)
    lane = lax.broadcasted_iota(jnp.int32, (1, POOL_DIM), 1)
    g0, g1, g2 = lane < POOL_GROUP, lane < 2 * POOL_GROUP, lane < 3 * POOL_GROUP
    width = jnp.where(g0, 2, jnp.where(g1, 4, jnp.where(g2, 8, 16)))
    pos = pos0 + t * seg + lax.broadcasted_iota(jnp.int32, (seg, 1), 0)
    cnt = jnp.minimum(width, pos + 1).astype(F32)

    def per_segment(j, carry):
        @pl.when(t == 0)
        def _():
            carry_sc[j] = hist_ref[j]

        rows = pl.ds(pl.multiple_of(j * seg, seg), seg)
        u = u_sc[rows, :]
        ext = jnp.concatenate([carry_sc[j], u], axis=0)
        s2 = ext + pltpu.roll(ext, 1, 0)
        s4 = s2 + pltpu.roll(s2, 2, 0)
        s8 = s4 + pltpu.roll(s4, 4, 0)
        s16 = s8 + pltpu.roll(s8, 8, 0)
        wsum = jnp.where(g0, s2, jnp.where(g1, s4, jnp.where(g2, s8, s16)))[HIST_ROWS:]
        m = wsum / cnt - u
        out = _dot(m.astype(BF), pw_ref[...]) * pscale_ref[...]
        pool_ref[rows, :] = out.astype(pool_ref.dtype)
        last = ext[seg:seg + HIST_ROWS]
        carry_sc[j] = last
        tail_ref[j] = last
        return carry

    if nseg == 1:
        per_segment(0, 0)
    else:
        lax.fori_loop(0, nseg, per_segment, 0)


def _in_proj(x, g, w, hist, pool_w, pool_scale, seq_len, pos0, pool_dtype):
    n = x.shape[0]
    tm = _row_tile(n, 512)
    seg = min(seq_len, tm)
    nseg = tm // seg
    tiles_per_seq = seq_len // seg
    wtot = w.shape[1]
    per_tile = lambda i: (i, 0)
    fixed = lambda i: (0, 0)
    seqs = lambda i: (i // tiles_per_seq, 0, 0)
    widths = IN_OUT_WIDTHS[1:]
    return pl.pallas_call(
        functools.partial(_in_proj_kernel, pos0=pos0, seg=seg, nseg=nseg, tiles_per_seq=tiles_per_seq),
        grid=(n // tm,),
        in_specs=[pl.BlockSpec((tm, D_MODEL), per_tile),
                  pl.BlockSpec((1, D_MODEL), fixed),
                  pl.BlockSpec((D_MODEL, wtot), fixed),
                  pl.BlockSpec((nseg, HIST_ROWS, POOL_DIM), seqs),
                  pl.BlockSpec((POOL_DIM, POOL_DIM), fixed),
                  pl.BlockSpec((1, POOL_DIM), fixed)],
        out_specs=[pl.BlockSpec((tm, POOL_DIM), per_tile),
                   pl.BlockSpec((nseg, HIST_ROWS, POOL_DIM), seqs)]
                  + [pl.BlockSpec((tm, wd), per_tile) for wd in widths],
        out_shape=[jax.ShapeDtypeStruct((n, POOL_DIM), pool_dtype),
                   jax.ShapeDtypeStruct((n // seq_len, HIST_ROWS, POOL_DIM), F32)]
                  + [jax.ShapeDtypeStruct((n, wd), F32) for wd in widths],
        scratch_shapes=[pltpu.VMEM((tm, POOL_DIM), F32), pltpu.VMEM((nseg, HIST_ROWS, POOL_DIM), F32)],
        compiler_params=_cparams(1),
        name="in_proj",
    )(x, g, w, hist, pool_w, pool_scale)


def _expand_heads(x):
    q = x.shape[0]
    col = lax.broadcasted_iota(jnp.int32, (1, SSD_DIM), 1)
    out = jnp.zeros((q, SSD_DIM), F32)
    for h in range(SSD_HEADS):
        sel = (col >= h * SSD_HEADDIM) & (col < (h + 1) * SSD_HEADDIM)
        out = jnp.where(sel, jnp.broadcast_to(x[:, h:h + 1], (q, SSD_DIM)), out)
    return out


def _ssd_kernel(z_ref, xbc_ref, dt_ref, hist_ref, h0_ref, convw_ref, convb_ref, dtb_ref, alog_ref,
                dskip_ref, ng_ref, y_ref, newconv_ref, ht_ref, ext_sc, hbig_sc, *, rows, bb):
    c = pl.program_id(1)
    q = SSD_CHUNK
    pad = q - rows

    def padded(x):
        if pad == 0:
            return x
        return jnp.concatenate([x, jnp.zeros((pad, x.shape[1]), x.dtype)], axis=0)

    row = lax.broadcasted_iota(jnp.int32, (q, 1), 0)
    col_t = lax.broadcasted_iota(jnp.int32, (1, q), 1)
    causal = row >= col_t
    lane = lax.broadcasted_iota(jnp.int32, (1, LANES), 1)
    row384 = lax.broadcasted_iota(jnp.int32, (SSD_DIM, 1), 0)
    upper_rows = row384 >= SSD_DIM // SSD_GROUPS
    state_valid = jnp.where(upper_rows, 1, 0) == jnp.where(lane >= SSD_STATE, 1, 0)
    a_neg = -jnp.exp(alog_ref[...])

    def per_seq(i, carry):
        @pl.when(c == 0)
        def _():
            ext_sc[i, 0:CONV_ROWS, :] = hist_ref[i]
            hbig_sc[i] = jnp.zeros((SSD_DIM, LANES), F32)
            hbig_sc[i, :, 0:SSD_STATE] = h0_ref[i]
            h0 = hbig_sc[i]
            hbig_sc[i] = jnp.where(upper_rows, pltpu.roll(h0, SSD_STATE, 1), h0)

        ext_sc[i, CONV_ROWS:CONV_ROWS + q, :] = padded(xbc_ref[i])
        ext = ext_sc[i]
        conv = convb_ref[...] + convw_ref[SSD_CONV - 1:SSD_CONV, :] * ext[CONV_ROWS:]
        for back in range(1, SSD_CONV):
            k = SSD_CONV - 1 - back
            conv = conv + convw_ref[k:k + 1, :] * pltpu.roll(ext, back, 0)[CONV_ROWS:]
        tail = ext_sc[i, rows:rows + CONV_ROWS, :]
        newconv_ref[i] = tail
        ext_sc[i, 0:CONV_ROWS, :] = tail

        act = conv * jax.nn.sigmoid(conv)
        xs = act[:, :SSD_DIM]
        bmat = act[:, SSD_DIM:SSD_DIM + LANES]
        cmat = act[:, SSD_DIM + LANES:]

        dtr = padded(dt_ref[i]) + dtb_ref[...]
        dt = jnp.maximum(dtr, 0.0) + jnp.log(1.0 + jnp.exp(-jnp.abs(dtr)))
        dt = jnp.where(row < rows, dt, 0.0)
        a_cs = dt * a_neg
        k = 1
        while k < q:
            a_cs = a_cs + jnp.where(row >= k, pltpu.roll(a_cs, k, 0), 0.0)
            k *= 2
        a_cs_t = a_cs.T
        a_last = a_cs[q - 1:q, :]
        decay_to_end = jnp.exp(a_last - a_cs)
        decay_from_start = jnp.exp(a_cs)
        chunk_decay = jnp.exp(a_last)

        dt_t = dt.T
        xdec = xs * _expand_heads(dt * decay_to_end)

        bmat_b = bmat.astype(BF)
        cb = [_dot_nt(jnp.where((lane >= g * SSD_STATE) & (lane < (g + 1) * SSD_STATE), cmat, 0.0).astype(BF),
                      bmat_b) for g in range(SSD_GROUPS)]
        tiles = []
        for t in range(SSD_DIM // LANES):
            xt = xs[:, t * LANES:(t + 1) * LANES]
            yt = jnp.zeros((q, LANES), F32)
            for h in (2 * t, 2 * t + 1):
                seg = a_cs[:, h:h + 1] - a_cs_t[h:h + 1, :]
                lmat = jnp.exp(jnp.where(causal, seg, -jnp.inf))
                m = cb[h // (SSD_HEADS // SSD_GROUPS)] * lmat * dt_t[h:h + 1, :]
                own = (lane < SSD_HEADDIM) if h % 2 == 0 else (lane >= SSD_HEADDIM)
                yt = yt + _dot(m.astype(BF), jnp.where(own, xt, 0.0).astype(BF))
            tiles.append(yt)
        y = jnp.concatenate(tiles, axis=1)

        hbig = hbig_sc[i]
        y_off = _dot_nt(cmat.astype(BF), hbig.astype(BF)) * _expand_heads(decay_from_start)
        y = y + y_off + dskip_ref[...] * xs
        zf = padded(z_ref[i])
        gated = y * (zf * jax.nn.sigmoid(zf))
        y_ref[i] = _rms(gated, ng_ref[...])[:rows].astype(y_ref.dtype)

        new_states = _dot(xdec.T.astype(BF), bmat_b)
        cd_rows = _expand_heads(jnp.broadcast_to(chunk_decay, (q, LANES))).T[:, :LANES]
        hnew = hbig * cd_rows + jnp.where(state_valid, new_states, 0.0)
        hbig_sc[i] = hnew
        ht_ref[i] = jnp.where(upper_rows, pltpu.roll(hnew, SSD_STATE, 1), hnew)[:, :SSD_STATE]
        return carry

    if bb <= SSD_SEQ_UNROLL:
        for i in range(bb):
            per_seq(i, 0)
    else:
        lax.fori_loop(0, bb, per_seq, 0)


def _ssd(z, xbc, dt, hist, hbig0, convw, convb, dtb, alog, dskip, ng, bb, out_dtype):
    b, l, _ = z.shape
    rows = min(l, SSD_CHUNK)
    nc = l // rows
    seq = lambda i, c: (i, c, 0)
    fixed3 = lambda i, c: (i, 0, 0)
    par = lambda i, c: (0, 0)
    return pl.pallas_call(
        functools.partial(_ssd_kernel, rows=rows, bb=bb),
        grid=(b // bb, nc),
        in_specs=[pl.BlockSpec((bb, rows, SSD_DIM), seq),
                  pl.BlockSpec((bb, rows, SSD_CONV_DIM), seq),
                  pl.BlockSpec((bb, rows, LANES), seq),
                  pl.BlockSpec((bb, CONV_ROWS, SSD_CONV_DIM), fixed3),
                  pl.BlockSpec((bb, SSD_DIM, SSD_STATE), fixed3),
                  pl.BlockSpec((SSD_CONV, SSD_CONV_DIM), par),
                  pl.BlockSpec((1, SSD_CONV_DIM), par),
                  pl.BlockSpec((1, LANES), par),
                  pl.BlockSpec((1, LANES), par),
                  pl.BlockSpec((1, SSD_DIM), par),
                  pl.BlockSpec((1, SSD_DIM), par)],
        out_specs=[pl.BlockSpec((bb, rows, SSD_DIM), seq),
                   pl.BlockSpec((bb, CONV_ROWS, SSD_CONV_DIM), fixed3),
                   pl.BlockSpec((bb, SSD_DIM, SSD_STATE), fixed3)],
        out_shape=[jax.ShapeDtypeStruct((b, l, SSD_DIM), out_dtype),
                   jax.ShapeDtypeStruct((b, CONV_ROWS, SSD_CONV_DIM), F32),
                   jax.ShapeDtypeStruct((b, SSD_DIM, SSD_STATE), F32)],
        scratch_shapes=[pltpu.VMEM((bb, CONV_ROWS + SSD_CHUNK, SSD_CONV_DIM), F32),
                        pltpu.VMEM((bb, SSD_DIM, LANES), F32)],
        compiler_params=_cparams(2),
        name="ssd_mix",
    )(z, xbc, dt, hist, hbig0, convw, convb, dtb, alog, dskip, ng)


Q_NOPE_W = MLA_HEADS * MLA_NOPE
Q_PE_W = MLA_HEADS * LANES


def _mla_prep_kernel(cq_ref, ckv_ref, kpe_ref, kper_ref, cos_ref, sin_ref, qg_ref, kg_ref, wq_ref, wuk_ref,
                     q_ref, k_ref, ckvn_ref, kr_ref):
    hq = _rms(cq_ref[...], qg_ref[...]).astype(BF)
    q_nope = _dot(hq, wq_ref[:, :Q_NOPE_W])
    q_pe = _dot(hq, wq_ref[:, Q_NOPE_W:Q_NOPE_W + Q_PE_W])
    q_pe_rot = _dot(hq, wq_ref[:, Q_NOPE_W + Q_PE_W:])
    q_lat = _dot(q_nope.astype(BF), wuk_ref[...])
    cos = cos_ref[...]
    sin = sin_ref[...]
    for h in range(MLA_HEADS):
        sl = slice(h * LANES, (h + 1) * LANES)
        q_ref[h, :, 0:LANES] = q_lat[:, sl].astype(q_ref.dtype)
        q_ref[h, :, LANES:QK_WIDTH] = (q_pe[:, sl] * cos + q_pe_rot[:, sl] * sin).astype(q_ref.dtype)
    ckvn = _rms(ckv_ref[...], kg_ref[...])
    ckvn_ref[...] = ckvn
    kr = kpe_ref[...] * cos + kper_ref[...] * sin
    kr_ref[...] = kr[:, :MLA_ROPE]
    k_ref[:, 0:LANES] = ckvn.astype(k_ref.dtype)
    lane = lax.broadcasted_iota(jnp.int32, (1, LANES), 1)
    k_ref[:, LANES:QK_WIDTH] = jnp.where(lane == LANES - 1, 1.0, kr).astype(k_ref.dtype)


def _mla_prep(cq, ckv, kpe, kper, cos, sin, qg, kg, wq, wuk, qk_dtype):
    n = cq.shape[0]
    tm = _row_tile(n, 512)
    ntab = cos.shape[0] // tm
    rowblk = lambda i: (i, 0)
    tab = lambda i: (i % ntab, 0)
    par = lambda i: (0, 0)
    return pl.pallas_call(
        _mla_prep_kernel,
        grid=(n // tm,),
        in_specs=[pl.BlockSpec((tm, MLA_Q_LORA), rowblk),
                  pl.BlockSpec((tm, MLA_KV_LORA), rowblk),
                  pl.BlockSpec((tm, LANES), rowblk),
                  pl.BlockSpec((tm, LANES), rowblk),
                  pl.BlockSpec((tm, LANES), tab),
                  pl.BlockSpec((tm, LANES), tab),
                  pl.BlockSpec((1, MLA_Q_LORA), par),
                  pl.BlockSpec((1, MLA_KV_LORA), par),
                  pl.BlockSpec(wq.shape, par),
                  pl.BlockSpec(wuk.shape, par)],
        out_specs=[pl.BlockSpec((MLA_HEADS, tm, QK_WIDTH), lambda i: (0, i, 0)),
                   pl.BlockSpec((tm, QK_WIDTH), rowblk),
                   pl.BlockSpec((tm, MLA_KV_LORA), rowblk),
                   pl.BlockSpec((tm, MLA_ROPE), rowblk)],
        out_shape=[jax.ShapeDtypeStruct((MLA_HEADS, n, QK_WIDTH), qk_dtype),
                   jax.ShapeDtypeStruct((n, QK_WIDTH), qk_dtype),
                   jax.ShapeDtypeStruct((n, MLA_KV_LORA), F32),
                   jax.ShapeDtypeStruct((n, MLA_ROPE), F32)],
        compiler_params=_cparams(1),
        name="mla_prep",
    )(cq, ckv, kpe, kper, cos, sin, qg, kg, wq, wuk)


def _value_up_norm(o_lat, rows, wuv_ref, g_ref):
    out = jnp.zeros((rows, MLA_DIM), F32)
    for h in range(MLA_HEADS):
        out = out + _dot(o_lat[h * rows:(h + 1) * rows].astype(BF), wuv_ref[h])
    return _rms(out, g_ref[...])


def _attn_prompt_kernel(q_ref, k_ref, wuv_ref, g_ref, o_ref, m_sc, acc_sc, *, bq, bk, hu):
    i = pl.program_id(1)
    m_sc[...] = jnp.full(m_sc.shape, -jnp.inf, F32)
    acc_sc[...] = jnp.zeros(acc_sc.shape, F32)
    exp2_scale = MLA_SCALE * LOG2E

    def head_step(h, kb, mask):
        s = _dot_nt(q_ref[h], kb)
        if mask is not None:
            s = jnp.where(mask, s, -jnp.inf)
        m_old = m_sc[h]
        m_new = jnp.maximum(m_old, jnp.max(s, axis=1, keepdims=True))
        alpha = jnp.exp2((m_old - m_new) * exp2_scale)
        p = jnp.exp2((s - jnp.concatenate([m_new] * (s.shape[1] // LANES), axis=1)) * exp2_scale)
        acc_sc[h] = jnp.concatenate([alpha, alpha], axis=1) * acc_sc[h] + _dot(p.astype(BF), kb)
        m_sc[h] = m_new

    def kv_step(j, width, masked):
        start = pl.multiple_of(j * bk, bk)
        mask = None
        if masked:
            t_q = i * bq + lax.broadcasted_iota(jnp.int32, (bq, width), 0)
            t_k = start + lax.broadcasted_iota(jnp.int32, (bq, width), 1)
            mask = t_k <= t_q

        def head_group(gi, carry):
            kb = k_ref[pl.ds(start, width), :]
            for u in range(hu):
                head_step(gi * hu + u, kb, mask)
            return carry

        lax.fori_loop(0, MLA_HEADS // hu, head_group, 0)

    n_full = (i * bq) // bk

    def body(j, carry):
        kv_step(j, bk, False)
        return carry

    lax.fori_loop(0, n_full, body, 0)
    rest = (i + 1) * bq - n_full * bk
    for width in range(bq, bk + 1, bq):
        @pl.when(rest == width)
        def _():
            kv_step(n_full, width, True)
    out = jnp.zeros((bq, MLA_DIM), F32)
    for h in range(MLA_HEADS):
        acc = acc_sc[h]
        o_lat = acc[:, :MLA_KV_LORA] / acc[:, QK_WIDTH - 1:QK_WIDTH]
        out = out + _dot(o_lat.astype(BF), wuv_ref[h])
    o_ref[...] = _rms(out, g_ref[...]).astype(o_ref.dtype)


def _attn_prompt(q, k, wuv, g, batch, out_dtype):
    n = k.shape[0]
    l = n // batch
    bq = _row_tile(l, ATTN_BQ)
    bk = _row_tile(l, ATTN_BK)
    nq = l // bq
    return pl.pallas_call(
        functools.partial(_attn_prompt_kernel, bq=bq, bk=bk, hu=ATTN_HEAD_UNROLL),
        grid=(batch, nq),
        in_specs=[pl.BlockSpec((MLA_HEADS, bq, QK_WIDTH), lambda b, i: (0, b * nq + i, 0)),
                  pl.BlockSpec((l, QK_WIDTH), lambda b, i: (b, 0)),
                  pl.BlockSpec(wuv.shape, lambda b, i: (0, 0, 0)),
                  pl.BlockSpec((1, MLA_DIM), lambda b, i: (0, 0))],
        out_specs=pl.BlockSpec((bq, MLA_DIM), lambda b, i: (b * nq + i, 0)),
        out_shape=jax.ShapeDtypeStruct((n, MLA_DIM), out_dtype),
        scratch_shapes=[pltpu.VMEM((MLA_HEADS, bq, LANES), F32), pltpu.VMEM((MLA_HEADS, bq, QK_WIDTH), F32)],
        compiler_params=_cparams(2),
        name="attn_prompt",
    )(q, k, wuv, g)


def _attn_sample_kernel(pt_ref, q_ref, knew_ref, ckv_hbm, kr_hbm, wuv_ref, g_ref, o_ref,
                        kbuf, rbuf, sems, *, layer, n_pages, t_new):
    b = pl.program_id(0)
    nb = pl.num_programs(0)
    past = n_pages * PAGE_SIZE

    def start_fetch(seq, slot):
        for j in range(n_pages):
            pg = pt_ref[seq, j]
            rows = pl.ds(j * PAGE_SIZE, PAGE_SIZE)
            pltpu.make_async_copy(ckv_hbm.at[layer, pg], kbuf.at[slot, rows, :], sems.at[0, slot]).start()
            pltpu.make_async_copy(kr_hbm.at[layer, pg], rbuf.at[slot, :, rows], sems.at[1, slot]).start()

    def wait_fetch(slot):
        pltpu.make_async_copy(kbuf.at[1 - slot], kbuf.at[slot], sems.at[0, slot]).wait()
        pltpu.make_async_copy(rbuf.at[1 - slot], rbuf.at[slot], sems.at[1, slot]).wait()

    @pl.when(b == 0)
    def _():
        start_fetch(0, 0)

    @pl.when(b + 1 < nb)
    def _():
        start_fetch(b + 1, (b + 1) % 2)

    slot = b % 2
    wait_fetch(slot)

    rows = MLA_HEADS * t_new
    exp2_scale = MLA_SCALE * LOG2E
    q = q_ref[...].reshape(rows, QK_WIDTH).astype(BF)
    q_lat = q[:, :MLA_KV_LORA]
    q_pe = q[:, MLA_KV_LORA:MLA_KV_LORA + MLA_ROPE]

    kn = jnp.concatenate([knew_ref[...], jnp.zeros((LANES - t_new, QK_WIDTH), knew_ref.dtype)],
                         axis=0).astype(BF)
    sn = _dot_nt(q, kn)
    t_q = lax.broadcasted_iota(jnp.int32, (1, t_new, LANES), 1)
    t_k = lax.broadcasted_iota(jnp.int32, (1, t_new, LANES), 2)
    sn = jnp.where(t_k <= t_q, sn.reshape(MLA_HEADS, t_new, LANES), -jnp.inf).reshape(rows, LANES)
    kf = kbuf[slot].astype(BF)
    rf = rbuf[slot].astype(BF)
    s = _dot_nt(q_lat, kf) + _dot(q_pe, rf)
    m = jnp.maximum(jnp.max(s, axis=1, keepdims=True), jnp.max(sn, axis=1, keepdims=True))
    p = jnp.exp2((s - m) * exp2_scale)
    pn = jnp.exp2((sn - m) * exp2_scale)
    denom = jnp.sum(p, axis=1, keepdims=True) + jnp.sum(pn, axis=1, keepdims=True)
    o_lat = (_dot(p.astype(BF), kf) + _dot(pn.astype(BF), kn[:, :MLA_KV_LORA])) / denom
    o_ref[...] = _value_up_norm(o_lat, t_new, wuv_ref, g_ref).astype(o_ref.dtype)


def _attn_sample(page_table, q, knew, cache_kv, cache_kr, wuv, g, layer, out_dtype):
    nseq, n_pages = page_table.shape
    n = knew.shape[0]
    t_new = n // nseq
    past = n_pages * PAGE_SIZE
    grid_spec = pltpu.PrefetchScalarGridSpec(
        num_scalar_prefetch=1,
        grid=(nseq,),
        in_specs=[pl.BlockSpec((MLA_HEADS, t_new, QK_WIDTH), lambda b, pt: (0, b, 0)),
                  pl.BlockSpec((t_new, QK_WIDTH), lambda b, pt: (b, 0)),
                  pl.BlockSpec(memory_space=pl.ANY),
                  pl.BlockSpec(memory_space=pl.ANY),
                  pl.BlockSpec(wuv.shape, lambda b, pt: (0, 0, 0)),
                  pl.BlockSpec((1, MLA_DIM), lambda b, pt: (0, 0))],
        out_specs=pl.BlockSpec((t_new, MLA_DIM), lambda b, pt: (b, 0)),
        scratch_shapes=[pltpu.VMEM((2, past, MLA_KV_LORA), F32),
                        pltpu.VMEM((2, MLA_ROPE, past), F32),
                        pltpu.SemaphoreType.DMA((2, 2))],
    )
    return pl.pallas_call(
        functools.partial(_attn_sample_kernel, layer=layer, n_pages=n_pages, t_new=t_new),
        grid_spec=grid_spec,
        out_shape=jax.ShapeDtypeStruct((n, MLA_DIM), out_dtype),
        compiler_params=_cparams(1),
        name="attn_sample",
    )(page_table, q, knew, cache_kv, cache_kr, wuv, g)


def _out_ffn_kernel(x_ref, pool_ref, ssd_ref, mla_ref, wout_ref, g_ref, wup_ref, wdn_ref, gf_ref, o_ref,
                    xn_sc, hn_sc, acc_sc, *, final):
    k = pl.program_id(1)

    @pl.when(k == 0)
    def _():
        y = (_dot(pool_ref[...].astype(BF), wout_ref[0:POOL_DIM])
             + _dot(ssd_ref[...].astype(BF), wout_ref[POOL_DIM:POOL_DIM + SSD_DIM])
             + _dot(mla_ref[...].astype(BF), wout_ref[POOL_DIM + SSD_DIM:]))
        xn = x_ref[...] + y
        xn_sc[...] = xn
        hn_sc[...] = _rms(xn, g_ref[...]).astype(BF)
        acc_sc[...] = jnp.zeros_like(acc_sc)

    a = jnp.maximum(_dot(hn_sc[...], wup_ref[...]), 0.0)
    acc_sc[...] += _dot((a * a).astype(BF), wdn_ref[...])

    @pl.when(k == pl.num_programs(1) - 1)
    def _():
        x_out = xn_sc[...] + acc_sc[...]
        o_ref[...] = _rms(x_out, gf_ref[...]) if final else x_out


def _out_ffn(x, pool, ssd, mla, wout, g, wup, wdn, gfinal, final):
    n = x.shape[0]
    tm = _row_tile(n, FFN_ROWS)
    tf = FFN_COLS
    rowblk = lambda i, k: (i, 0)
    par = lambda i, k: (0, 0)
    return pl.pallas_call(
        functools.partial(_out_ffn_kernel, final=final),
        grid=(n // tm, FFN_DIM // tf),
        in_specs=[pl.BlockSpec((tm, D_MODEL), rowblk),
                  pl.BlockSpec((tm, POOL_DIM), rowblk),
                  pl.BlockSpec((tm, SSD_DIM), rowblk),
                  pl.BlockSpec((tm, MLA_DIM), rowblk),
                  pl.BlockSpec((D_MODEL, D_MODEL), par),
                  pl.BlockSpec((1, D_MODEL), par),
                  pl.BlockSpec((D_MODEL, tf), lambda i, k: (0, k)),
                  pl.BlockSpec((tf, D_MODEL), lambda i, k: (k, 0)),
                  pl.BlockSpec((1, D_MODEL), par)],
        out_specs=pl.BlockSpec((tm, D_MODEL), rowblk),
        out_shape=jax.ShapeDtypeStruct((n, D_MODEL), F32),
        scratch_shapes=[pltpu.VMEM((tm, D_MODEL), F32), pltpu.VMEM((tm, D_MODEL), BF),
                        pltpu.VMEM((tm, D_MODEL), F32)],
        compiler_params=_cparams(2),
        name="out_ffn",
    )(x, pool, ssd, mla, wout, g, wup, wdn, gfinal)


def _pad_cols(a, width):
    return jnp.pad(a, [(0, 0)] * (a.ndim - 1) + [(0, width - a.shape[-1])])


def _rot_half_cols(w):
    half = MLA_ROPE // 2
    return jnp.concatenate([-w[..., half:], w[..., :half]], axis=-1)


def _prep_weights(w_in, pool_w, w_uq, w_uk, w_uv, w_out, w_up, w_down):
    depth = w_in.shape[0]
    parts, start = [], 0
    for wd in IN_WIDTHS:
        parts.append(w_in[..., start:start + wd])
        start += wd
    u, z, xbc, dt, cq, ckv, kpe = parts
    w_in_p = jnp.concatenate([u, z, xbc, cq, ckv, _pad_cols(kpe, LANES), _pad_cols(_rot_half_cols(kpe), LANES),
                              _pad_cols(dt, LANES)], axis=-1).astype(BF)
    eye4 = jnp.eye(len(POOL_WINDOWS), dtype=F32)
    pool_bd = (pool_w[:, :, :, None, :] * eye4[None, :, None, :, None]).reshape(depth, POOL_DIM, POOL_DIM).astype(BF)
    wq = w_uq.reshape(depth, MLA_Q_LORA, MLA_HEADS, MLA_NOPE + MLA_ROPE)
    wq_nope = wq[..., :MLA_NOPE].reshape(depth, MLA_Q_LORA, Q_NOPE_W)
    wq_pe = wq[..., MLA_NOPE:]
    wq_p = jnp.concatenate([wq_nope,
                            _pad_cols(wq_pe, LANES).reshape(depth, MLA_Q_LORA, Q_PE_W),
                            _pad_cols(_rot_half_cols(wq_pe), LANES).reshape(depth, MLA_Q_LORA, Q_PE_W)],
                           axis=-1).astype(BF)
    eye6 = jnp.eye(MLA_HEADS, dtype=F32)
    uk = jnp.transpose(w_uk, (0, 2, 3, 1))
    wuk_bd = (uk[:, :, :, None, :] * eye6[None, :, None, :, None]).reshape(
        depth, Q_NOPE_W, MLA_HEADS * MLA_KV_LORA).astype(BF)
    uv = jnp.transpose(w_uv, (0, 2, 1, 3))
    wuv_pad = (uv[:, :, :, None, :] * eye6[None, :, None, :, None]).reshape(
        depth, MLA_HEADS, MLA_KV_LORA, MLA_DIM).astype(BF)
    return w_in_p, pool_bd, wq_p, wuk_bd, wuv_pad, w_out.astype(BF), w_up.astype(BF), w_down.astype(BF)


def _rope_tables(pos, reps):
    half = MLA_ROPE // 2
    inv = ROPE_THETA ** (-jnp.arange(half, dtype=F32) * (2.0 / MLA_ROPE))
    ang = pos.astype(F32)[:, None] * inv[None, :]
    cos = jnp.cos(ang)
    sin = jnp.sin(ang)
    cos = _pad_cols(jnp.concatenate([cos, cos], axis=-1), LANES)
    sin = _pad_cols(jnp.concatenate([sin, sin], axis=-1), LANES)
    return jnp.tile(cos, (reps, 1)), jnp.tile(sin, (reps, 1))


def _state_to_rows(h):
    return h.reshape(h.shape[0], SSD_DIM, SSD_STATE)


def _group_layer(x, batch, lw, pos0, pool_hist, conv_hist, ssm_h0, tables, attend, bb, act_dtype):
    n = x.shape[0]
    l = n // batch
    pool_out, pool_tail, z, xbc, cq, ckv, kpe, kper, dt = _in_proj(
        x, lw["norm_mix_g"], lw["w_in"], pool_hist, lw["pool_bd"], lw["pool_scale"], l, pos0, act_dtype)
    ssd_out, conv_tail, hbig = _ssd(z.reshape(batch, l, SSD_DIM), xbc.reshape(batch, l, SSD_CONV_DIM),
                                    dt.reshape(batch, l, LANES), conv_hist, ssm_h0,
                                    lw["conv_w"], lw["conv_b"], lw["dt_bias"], lw["a_log"], lw["d_skip"],
                                    lw["ssd_norm_g"], bb, act_dtype)
    q, kcat, ckv_n, kpe_r = _mla_prep(cq, ckv, kpe, kper, tables[0], tables[1], lw["q_norm_g"], lw["kv_norm_g"],
                                      lw["wq"], lw["wuk_bd"], act_dtype)
    mla_out = attend(q, kcat)
    x_new = _out_ffn(x, pool_out, ssd_out.reshape(n, SSD_DIM), mla_out,
                     lw["w_out"], lw["norm_ffn_g"], lw["w_up"], lw["w_down"], lw["final_g"], lw["final"])
    new_pool = pool_tail[:, HIST_ROWS - POOL_HIST:]
    new_conv = conv_tail[:, CONV_ROWS - (SSD_CONV - 1):]
    new_ssm = hbig.reshape(batch, SSD_HEADS, SSD_HEADDIM, SSD_STATE)
    return (x_new, ckv_n.reshape(batch, l, MLA_KV_LORA), kpe_r.reshape(batch, l, MLA_ROPE),
            new_pool, new_conv, new_ssm)


def kernel(x_prompt, x_sample, cache_kv_latent, cache_k_rope, state_pool, state_conv, state_ssm, page_table,
           norm_mix_g, w_in, pool_w, pool_scale, conv_w, conv_b, dt_bias, a_log, d_skip, ssd_norm_g, q_norm_g,
           w_uq, kv_norm_g, w_uk, w_uv, mla_out_g, w_out, norm_ffn_g, w_up, w_down, final_norm_g):
    depth = w_in.shape[0]
    bp, seq, _ = x_prompt.shape
    db, dec_seq, _ = x_sample.shape
    past_len = page_table.shape[1] * PAGE_SIZE

    w_in_p, pool_bd, wq_p, wuk_bd, wuv_pad, w_out_b, w_up_b, w_down_b = _prep_weights(
        w_in, pool_w, w_uq, w_uk, w_uv, w_out, w_up, w_down)
    tables_p = _rope_tables(jnp.arange(seq, dtype=jnp.int32), 1)
    tile_s = _row_tile(db * dec_seq, 512)
    tables_s = _rope_tables(past_len + jnp.arange(dec_seq, dtype=jnp.int32), tile_s // dec_seq)

    cache_kr_t = jnp.swapaxes(cache_k_rope, 2, 3)

    xp = x_prompt.reshape(bp * seq, D_MODEL)
    xs = x_sample.reshape(db * dec_seq, D_MODEL)
    zero_pool = jnp.zeros((bp, HIST_ROWS, POOL_DIM), F32)
    zero_conv = jnp.zeros((bp, CONV_ROWS, SSD_CONV_DIM), F32)
    zero_ssm = jnp.zeros((bp, SSD_DIM, SSD_STATE), F32)
    outs_p, outs_s = [], []
    for l in range(depth):
        lw = {
            "norm_mix_g": norm_mix_g[l][None], "w_in": w_in_p[l], "pool_bd": pool_bd[l],
            "pool_scale": pool_scale[l][None], "conv_w": conv_w[l], "conv_b": conv_b[l][None],
            "dt_bias": _pad_cols(dt_bias[l][None], LANES), "a_log": _pad_cols(a_log[l][None], LANES),
            "d_skip": jnp.repeat(d_skip[l], SSD_HEADDIM)[None], "ssd_norm_g": ssd_norm_g[l][None],
            "q_norm_g": q_norm_g[l][None], "kv_norm_g": kv_norm_g[l][None], "wq": wq_p[l], "wuk_bd": wuk_bd[l],
            "w_out": w_out_b[l], "norm_ffn_g": norm_ffn_g[l][None], "w_up": w_up_b[l], "w_down": w_down_b[l],
            "final_g": final_norm_g[None], "final": l == depth - 1,
        }
        wuv_l = wuv_pad[l]
        og = mla_out_g[l][None]

        attend_p = lambda q, k: _attn_prompt(q, k, wuv_l, og, bp, BF)
        res = _group_layer(xp, bp, lw, 0, zero_pool, zero_conv, zero_ssm, tables_p, attend_p, min(bp, SSD_SEQ_UNROLL), BF)
        xp = res[0]
        outs_p.append(res[1:])

        attend_s = lambda q, k: _attn_sample(page_table, q, k, cache_kv_latent, cache_kr_t, wuv_l, og, l, F32)
        pool_hist = jnp.pad(state_pool[l], ((0, 0), (HIST_ROWS - POOL_HIST, 0), (0, 0)))
        conv_hist = jnp.pad(state_conv[l], ((0, 0), (CONV_ROWS - (SSD_CONV - 1), 0), (0, 0)))
        res = _group_layer(xs, db, lw, past_len, pool_hist, conv_hist, _state_to_rows(state_ssm[l]),
                           tables_s, attend_s, 8, F32)
        xs = res[0]
        outs_s.append(res[1:])

    y_prompt = xp.reshape(bp, seq, D_MODEL)
    y_sample = xs.reshape(db, dec_seq, D_MODEL)
    stack = lambda outs, k: jnp.stack([o[k] for o in outs])
    return (y_prompt, y_sample,
            stack(outs_p, 0), stack(outs_p, 1), stack(outs_p, 2), stack(outs_p, 3), stack(outs_p, 4),
            stack(outs_s, 0), stack(outs_s, 1), stack(outs_s, 2), stack(outs_s, 3), stack(outs_s, 4))
```
